```python
import jax, jax.numpy as jnp
from jax import lax
import numpy as np

D_MODEL = 1024
BATCH = 4
SEQ = 8192
DEPTH = 2
DEC_BATCH = 128
DEC_SEQ = 1
PAST_LEN = 16384
PAGE_SIZE = 128

ROPE_THETA = 10000.0
NORM_EPS = 1e-6
Q_BLOCK = 128
N_BRANCH = 4
H_A = 4
HKV_A = 2
HD_A = 64
W_A = H_A * HD_A
MOBA_BLOCK = 256
MOBA_TOPK = 3
H_B = 4
Q_LORA = 256
KV_LORA = 128
NOPE_B = 64
ROPE_B = 32
V_B = 64
W_B = H_B * V_B
H_C = 4
HS_C = 64
W_C = H_C * HS_C
DECAY_LORA = 64
ICL_LORA = 64
C_SHIFT = 3 * W_C + DECAY_LORA + ICL_LORA
GN_EPS = 64e-5
G_D = 4
W_D = 256
CHUNK_D = 128

IN_SPLITS = (
    ('a_q', H_A * HD_A), ('a_k', HKV_A * HD_A), ('a_v', HKV_A * HD_A), ('a_g', W_A),
    ('b_dq', Q_LORA), ('b_dkv', KV_LORA + ROPE_B), ('b_g', W_B),
    ('c_x', C_SHIFT), ('c_g', W_C),
    ('d_u', W_D), ('d_v', W_D), ('d_g', W_D),
    ('m_g', N_BRANCH * D_MODEL),
)
IN_COLS = sum(w for _, w in IN_SPLITS)

kernel_name = 'hybrid_moba_mla_rwkv7_gmlp_step'


def rms_norm(x, g, eps=NORM_EPS):
    xf = x.astype(jnp.float32)
    y = xf * lax.rsqrt(jnp.mean(xf * xf, axis=-1, keepdims=True) + eps)
    return (y * g.astype(jnp.float32)).astype(x.dtype)


def rope(x, pos):
    half = x.shape[-1] // 2
    inv = ROPE_THETA ** (-jnp.arange(half, dtype=jnp.float32) / half)
    ang = pos.astype(jnp.float32)[:, None] * inv[None, :]
    cos = jnp.cos(ang)[:, None, :]
    sin = jnp.sin(ang)[:, None, :]
    xf = x.astype(jnp.float32)
    x1, x2 = xf[..., :half], xf[..., half:]
    return jnp.concatenate([x1 * cos - x2 * sin, x2 * cos + x1 * sin], axis=-1).astype(x.dtype)


def project(x, g, w):
    h = rms_norm(x, g)
    p = jnp.einsum('bsd,dc->bsc', h, w)
    cols = {}
    o = 0
    for name, width in IN_SPLITS:
        cols[name] = p[..., o:o + width]
        o += width
    return cols


def a_qkv(cols, pos, q_norm, k_norm):
    B, T = cols['a_q'].shape[:2]
    q = rope(rms_norm(cols['a_q'].reshape(B, T, H_A, HD_A), q_norm), pos)
    k = rope(rms_norm(cols['a_k'].reshape(B, T, HKV_A, HD_A), k_norm), pos)
    v = cols['a_v'].reshape(B, T, HKV_A, HD_A)
    return q, k, v


def moba_prompt(q, k, v):
    B, S = q.shape[:2]
    nb = -(-S // MOBA_BLOCK)
    pad = nb * MOBA_BLOCK - S
    topk = min(MOBA_TOPK, nb - 1)
    kv_of_h = jnp.arange(H_A) // (H_A // HKV_A)
    scale = HD_A ** -0.5
    qh = q.transpose(0, 2, 1, 3)

    def to_blocks(t):
        t = jnp.pad(t, ((0, 0), (0, pad), (0, 0), (0, 0)))
        return t.transpose(0, 2, 1, 3).reshape(B, HKV_A, nb, MOBA_BLOCK, HD_A)

    kb, vb = to_blocks(k), to_blocks(v)
    kmean = jnp.mean(kb.astype(jnp.float32), axis=3)[:, kv_of_h]
    bi = jnp.arange(B)[:, None, None, None]
    hi = kv_of_h[None, :, None, None]

    def one_block(i):
        s0 = i * Q_BLOCK
        own = s0 // MOBA_BLOCK
        qpos = s0 + jnp.arange(Q_BLOCK)
        qi = lax.dynamic_slice_in_dim(qh, s0, Q_BLOCK, axis=2).astype(jnp.float32)
        k_own = lax.dynamic_index_in_dim(kb, own, axis=2, keepdims=False)[:, kv_of_h]
        v_own = lax.dynamic_index_in_dim(vb, own, axis=2, keepdims=False)[:, kv_of_h]
        kpos = own * MOBA_BLOCK + jnp.arange(MOBA_BLOCK)
        s_own = jnp.einsum('bhqd,bhkd->bhqk', qi, k_own) * scale
        s_own = jnp.where(kpos[None, :] <= qpos[:, None], s_own, -jnp.inf)
        if topk == 0:
            p = jax.nn.softmax(s_own, axis=-1).astype(v.dtype)
            return jnp.einsum('bhqk,bhkd->bhqd', p, v_own)
        gs = jnp.einsum('bhqd,bhnd->bhqn', qi, kmean)
        gs = jnp.where(jnp.arange(nb) < own, gs, -jnp.inf)
        _, idx = lax.top_k(gs, topk)
        valid = jnp.arange(topk) < own
        k_sel = kb[bi, hi, idx]
        v_sel = vb[bi, hi, idx].reshape(B, H_A, Q_BLOCK, topk * MOBA_BLOCK, HD_A)
        s_sel = jnp.einsum('bhqd,bhqjkd->bhqjk', qi, k_sel) * scale
        s_sel = jnp.where(valid[:, None], s_sel, -jnp.inf).reshape(B, H_A, Q_BLOCK, topk * MOBA_BLOCK)
        p = jax.nn.softmax(jnp.concatenate([s_sel, s_own], axis=-1), axis=-1).astype(v.dtype)
        n_sel = topk * MOBA_BLOCK
        return (jnp.einsum('bhqk,bhqkd->bhqd', p[..., :n_sel], v_sel)
                + jnp.einsum('bhqk,bhkd->bhqd', p[..., n_sel:], v_own))

    o = lax.map(one_block, jnp.arange(S // Q_BLOCK))
    return o.transpose(1, 0, 3, 2, 4).reshape(B, S, W_A)


def moba_sample(q, k_new, v_new, cache_k, cache_v, page_table):
    DB, T = q.shape[:2]
    P = page_table.shape[1] * PAGE_SIZE
    n_full = P // MOBA_BLOCK
    own_start = n_full * MOBA_BLOCK
    l_own = P - own_start
    ppb = MOBA_BLOCK // PAGE_SIZE
    kv_of_h = jnp.arange(H_A) // (H_A // HKV_A)
    scale = HD_A ** -0.5
    qh = q.transpose(0, 2, 1, 3).astype(jnp.float32)
    qpos = P + jnp.arange(T)
    own_phys = page_table[:, own_start // PAGE_SIZE:]
    k_own = jnp.concatenate([cache_k[own_phys].reshape(DB, l_own, HKV_A, HD_A), k_new], axis=1)
    v_own = jnp.concatenate([cache_v[own_phys].reshape(DB, l_own, HKV_A, HD_A), v_new], axis=1)
    k_own = k_own[:, :, kv_of_h].transpose(0, 2, 1, 3)
    v_own = v_own[:, :, kv_of_h].transpose(0, 2, 1, 3)
    kpos = jnp.concatenate([jnp.arange(own_start, P), qpos])
    mask = (kpos[None, :] <= qpos[:, None]) & (kpos[None, :] // MOBA_BLOCK == qpos[:, None] // MOBA_BLOCK)
    s_own = jnp.where(mask, jnp.einsum('bhtd,bhkd->bhtk', qh, k_own) * scale, -jnp.inf)
    topk = min(MOBA_TOPK, n_full)
    if topk == 0:
        p = jax.nn.softmax(s_own, axis=-1).astype(v_new.dtype)
        o = jnp.einsum('bhtk,bhkd->bhtd', p, v_own)
        return o.transpose(0, 2, 1, 3).reshape(DB, T, W_A)
    k_past = cache_k[page_table[:, :n_full * ppb]].reshape(DB, n_full, MOBA_BLOCK, HKV_A, HD_A)
    kmean = jnp.mean(k_past.astype(jnp.float32), axis=2)[:, :, kv_of_h]
    gs = jnp.einsum('bhtd,bnhd->bhtn', qh, kmean)
    _, idx = lax.top_k(gs, topk)
    lpage = idx[..., None] * ppb + jnp.arange(ppb)
    phys = page_table[jnp.arange(DB)[:, None, None, None, None], lpage]
    gi = (phys[..., None], jnp.arange(PAGE_SIZE), kv_of_h[None, :, None, None, None, None])
    k_sel = cache_k[gi].reshape(DB, H_A, T, topk * MOBA_BLOCK, HD_A)
    v_sel = cache_v[gi].reshape(DB, H_A, T, topk * MOBA_BLOCK, HD_A)
    s_sel = jnp.einsum('bhtd,bhtkd->bhtk', qh, k_sel) * scale
    p = jax.nn.softmax(jnp.concatenate([s_sel, s_own], axis=-1), axis=-1).astype(v_new.dtype)
    n_sel = topk * MOBA_BLOCK
    o = (jnp.einsum('bhtk,bhtkd->bhtd', p[..., :n_sel], v_sel)
         + jnp.einsum('bhtk,bhkd->bhtd', p[..., n_sel:], v_own))
    return o.transpose(0, 2, 1, 3).reshape(DB, T, W_A)


def mla_qkv(cols, pos, q_a_norm, w_uq, q_norm, kv_a_norm, kpe_norm):
    cq = rms_norm(cols['b_dq'], q_a_norm)
    q = rms_norm(jnp.einsum('btc,chd->bthd', cq, w_uq), q_norm)
    q_nope = q[..., :NOPE_B]
    q_pe = rope(q[..., NOPE_B:], pos)
    dkv = cols['b_dkv']
    lat = rms_norm(dkv[..., :KV_LORA], kv_a_norm)
    kpe = rope(rms_norm(dkv[:, :, None, KV_LORA:], kpe_norm), pos)[:, :, 0, :]
    return q_nope, q_pe, lat, kpe


def mla_prompt(q_nope, q_pe, lat, kpe, w_uk, w_uv):
    B, S = lat.shape[:2]
    k_nope = jnp.einsum('bsc,chd->bshd', lat, w_uk)
    v = jnp.einsum('bsc,chd->bshd', lat, w_uv)
    kpos = jnp.arange(S)
    scale = (NOPE_B + ROPE_B) ** -0.5

    def one_block(i):
        s0 = i * Q_BLOCK
        qn = lax.dynamic_slice_in_dim(q_nope, s0, Q_BLOCK, axis=1)
        qp = lax.dynamic_slice_in_dim(q_pe, s0, Q_BLOCK, axis=1)
        s = (jnp.einsum('bqhd,bkhd->bhqk', qn, k_nope, preferred_element_type=jnp.float32)
             + jnp.einsum('bqhd,bkd->bhqk', qp, kpe, preferred_element_type=jnp.float32)) * scale
        qpos = s0 + jnp.arange(Q_BLOCK)
        s = jnp.where(kpos[None, :] <= qpos[:, None], s, -jnp.inf)
        p = jax.nn.softmax(s, axis=-1).astype(v.dtype)
        return jnp.einsum('bhqk,bkhd->bqhd', p, v)

    o = lax.map(one_block, jnp.arange(S // Q_BLOCK))
    return o.transpose(1, 0, 2, 3, 4).reshape(B, S, W_B)


def mla_sample(q_nope, q_pe, lat, kpe, cache_lat, cache_kpe, page_table, w_uk, w_uv):
    DB, T = lat.shape[:2]
    P = page_table.shape[1] * PAGE_SIZE
    lat_all = jnp.concatenate([cache_lat[page_table].reshape(DB, P, KV_LORA), lat], axis=1)
    kpe_all = jnp.concatenate([cache_kpe[page_table].reshape(DB, P, ROPE_B), kpe], axis=1)
    scale = (NOPE_B + ROPE_B) ** -0.5
    q_lat = jnp.einsum('bthd,chd->bthc', q_nope, w_uk)
    s = (jnp.einsum('bthc,bkc->bhtk', q_lat, lat_all, preferred_element_type=jnp.float32)
         + jnp.einsum('bthd,bkd->bhtk', q_pe, kpe_all, preferred_element_type=jnp.float32)) * scale
    kpos = jnp.arange(P + T)
    qpos = P + jnp.arange(T)
    s = jnp.where(kpos[None, :] <= qpos[:, None], s, -jnp.inf)
    p = jax.nn.softmax(s, axis=-1).astype(lat_all.dtype)
    o_lat = jnp.einsum('bhtk,bkc->bthc', p, lat_all)
    o = jnp.einsum('bthc,chd->bthd', o_lat, w_uv)
    return o.reshape(DB, T, W_B)


def wkv_scan(s0, r, w, k, v, kk, a):
    def step(S, inp):
        r_t, w_t, k_t, v_t, kk_t, a_t = inp
        sa = jnp.einsum('bhvk,bhk->bhv', S, -kk_t)
        S = (S * w_t[:, :, None, :] + sa[..., None] * (kk_t * a_t)[:, :, None, :]
             + v_t[..., None] * k_t[:, :, None, :])
        return S, jnp.einsum('bhvk,bhk->bhv', S, r_t)

    xs = tuple(t.transpose(1, 0, 2, 3) for t in (r, w, k, v, kk, a))
    S, ys = lax.scan(step, s0, xs)
    return S, ys.transpose(1, 0, 2, 3)


def rwkv_mix(xc, prev_row, s0, mu, w0, w_up, a0, a_up, k_k, k_a, r_k, gn):
    B, T = xc.shape[:2]
    f32 = jnp.float32
    prev = jnp.concatenate([prev_row[:, None].astype(xc.dtype), xc[:, :-1]], axis=1)
    xs = xc + (prev - xc) * mu
    r = xs[..., :W_C]
    k = xs[..., W_C:2 * W_C]
    v = xs[..., 2 * W_C:3 * W_C]
    wd = xs[..., 3 * W_C:3 * W_C + DECAY_LORA]
    ad = xs[..., 3 * W_C + DECAY_LORA:]
    w_log = -jax.nn.softplus(-(w0 + jnp.tanh(wd) @ w_up).astype(f32)) - 0.5
    decay = jnp.exp(-jnp.exp(w_log))
    a = jax.nn.sigmoid((a0 + ad @ a_up).astype(f32))
    heads = lambda t: t.astype(f32).reshape(B, T, H_C, HS_C)
    r, k, v, decay, a = heads(r), heads(k), heads(v), heads(decay), heads(a)
    kk = k * k_k.astype(f32).reshape(H_C, HS_C)
    kk = kk * lax.rsqrt(jnp.sum(kk * kk, axis=-1, keepdims=True) + 1e-12)
    k = k * (1.0 + (a - 1.0) * k_a.astype(f32).reshape(H_C, HS_C))
    S, y = wkv_scan(s0.astype(f32), r, decay, k, v, kk, a)
    mean = jnp.mean(y, axis=-1, keepdims=True)
    var = jnp.mean(jnp.square(y - mean), axis=-1, keepdims=True)
    y = (y - mean) * lax.rsqrt(var + GN_EPS) * gn.astype(f32).reshape(H_C, HS_C)
    y = y + jnp.sum(r * k * r_k.astype(f32), axis=-1, keepdims=True) * v
    return y.reshape(B, T, W_C).astype(xc.dtype), S.astype(xc.dtype), xc[:, -1]


def gmlp_mix(du, dv, v_norm, ws, b, chunk):
    B, T = du.shape[:2]
    u = jax.nn.gelu(du)
    v = rms_norm(jax.nn.gelu(dv), v_norm)
    tri = jnp.tril(jnp.ones((chunk, chunk), dtype=bool))
    w = jnp.where(tri, ws[:, :chunk, :chunk], 0.0)
    vc = v.reshape(B, T // chunk, chunk, G_D, W_D // G_D)
    sp = jnp.einsum('gts,bcsgd->bctgd', w, vc) + b[:, :chunk].T[None, None, :, :, None]
    return u * sp.reshape(B, T, W_D), v


def merge(x, m_g, o_a, o_b, o_c, o_d, w_br_a, w_br_b, w_br_c, w_br_d, w_out):
    B, T = x.shape[:2]
    g = jax.nn.sigmoid(m_g.reshape(B, T, N_BRANCH, D_MODEL))
    y = (g[:, :, 0] * (o_a @ w_br_a) + g[:, :, 1] * (o_b @ w_br_b)
         + g[:, :, 2] * (o_c @ w_br_c) + g[:, :, 3] * (o_d @ w_br_d))
    return x + y @ w_out


def setup_inputs(seed: int = 0) -> dict:
    key = jax.random.key(seed)
    ks = iter(jax.random.split(key, 64))
    f32 = jnp.float32
    nrm = lambda shape, scale: scale * jax.random.normal(next(ks), shape, f32)
    gain = lambda shape: 1.0 + 0.05 * jax.random.normal(next(ks), shape, f32)
    n_pages = PAST_LEN // PAGE_SIZE
    n_used = DEC_BATCH * n_pages
    n_pool = n_used + (n_used + 3) // 4
    page_table = jax.random.permutation(next(ks), n_pool)[:n_used].reshape(DEC_BATCH, n_pages).astype(jnp.int32)
    return {
        'x_prompt': nrm((BATCH, SEQ, D_MODEL), 1.0),
        'x_sample': nrm((DEC_BATCH, DEC_SEQ, D_MODEL), 1.0),
        'cache_a_k': nrm((DEPTH, n_pool, PAGE_SIZE, HKV_A, HD_A), 1.0),
        'cache_a_v': nrm((DEPTH, n_pool, PAGE_SIZE, HKV_A, HD_A), 1.0),
        'cache_b_latent': nrm((DEPTH, n_pool, PAGE_SIZE, KV_LORA), 1.0),
        'cache_b_kpe': nrm((DEPTH, n_pool, PAGE_SIZE, ROPE_B), 1.0),
        'state_c_wkv': nrm((DEPTH, DEC_BATCH, H_C, HS_C, HS_C), 1.0),
        'state_c_shift': nrm((DEPTH, DEC_BATCH, C_SHIFT), 1.0),
        'page_table': page_table,
        'ln_g': gain((DEPTH, D_MODEL)),
        'w_in': nrm((DEPTH, D_MODEL, IN_COLS), D_MODEL ** -0.5),
        'a_q_norm': gain((DEPTH, HD_A)),
        'a_k_norm': gain((DEPTH, HD_A)),
        'b_q_a_norm': gain((DEPTH, Q_LORA)),
        'b_w_uq': nrm((DEPTH, Q_LORA, H_B, NOPE_B + ROPE_B), Q_LORA ** -0.5),
        'b_q_norm': gain((DEPTH, NOPE_B + ROPE_B)),
        'b_kv_a_norm': gain((DEPTH, KV_LORA)),
        'b_kpe_norm': gain((DEPTH, ROPE_B)),
        'b_w_uk': nrm((DEPTH, KV_LORA, H_B, NOPE_B), KV_LORA ** -0.5),
        'b_w_uv': nrm((DEPTH, KV_LORA, H_B, V_B), KV_LORA ** -0.5),
        'c_mu': jax.random.uniform(next(ks), (DEPTH, C_SHIFT), f32),
        'c_w0': -1.0 + nrm((DEPTH, W_C), 0.5),
        'c_w_up': nrm((DEPTH, DECAY_LORA, W_C), 0.5 * DECAY_LORA ** -0.5),
        'c_a0': nrm((DEPTH, W_C), 0.5),
        'c_a_up': nrm((DEPTH, ICL_LORA, W_C), 0.5 * ICL_LORA ** -0.5),
        'c_k_k': 0.85 + nrm((DEPTH, W_C), 0.05),
        'c_k_a': gain((DEPTH, W_C)),
        'c_r_k': nrm((DEPTH, H_C, HS_C), 0.1),
        'c_gn': gain((DEPTH, W_C)),
        'd_v_norm': gain((DEPTH, W_D)),
        'd_ws': nrm((DEPTH, G_D, CHUNK_D, CHUNK_D), CHUNK_D ** -0.5),
        'd_b': 1.0 + nrm((DEPTH, G_D, CHUNK_D), 0.1),
        'w_br_a': nrm((DEPTH, W_A, D_MODEL), W_A ** -0.5),
        'w_br_b': nrm((DEPTH, W_B, D_MODEL), W_B ** -0.5),
        'w_br_c': nrm((DEPTH, W_C, D_MODEL), W_C ** -0.5),
        'w_br_d': nrm((DEPTH, W_D, D_MODEL), W_D ** -0.5),
        'w_out': nrm((DEPTH, D_MODEL, D_MODEL), D_MODEL ** -0.5),
    }


def reference(x_prompt, x_sample, cache_a_k, cache_a_v, cache_b_latent, cache_b_kpe,
              state_c_wkv, state_c_shift, page_table,
              ln_g, w_in, a_q_norm, a_k_norm, b_q_a_norm, b_w_uq, b_q_norm, b_kv_a_norm,
              b_kpe_norm, b_w_uk, b_w_uv, c_mu, c_w0, c_w_up, c_a0, c_a_up, c_k_k, c_k_a,
              c_r_k, c_gn, d_v_norm, d_ws, d_b, w_br_a, w_br_b, w_br_c, w_br_d, w_out):
    Bp, S = x_prompt.shape[:2]
    DB, T = x_sample.shape[:2]
    P = page_table.shape[1] * PAGE_SIZE
    pos_p = jnp.arange(S, dtype=jnp.int32)
    pos_s = P + jnp.arange(T, dtype=jnp.int32)
    silu = jax.nn.silu
    hp, hs = x_prompt, x_sample
    pa_k, pa_v, pb_lat, pb_kpe, pc_wkv, pc_shift = [], [], [], [], [], []
    sa_k, sa_v, sb_lat, sb_kpe, sc_wkv, sc_shift, sd_v = [], [], [], [], [], [], []
    for l in range(DEPTH):
        cp = project(hp, ln_g[l], w_in[l])
        qa, ka, va = a_qkv(cp, pos_p, a_q_norm[l], a_k_norm[l])
        o_a = moba_prompt(qa, ka, va) * silu(cp['a_g'])
        qn, qpe, lat, kpe = mla_qkv(cp, pos_p, b_q_a_norm[l], b_w_uq[l], b_q_norm[l], b_kv_a_norm[l], b_kpe_norm[l])
        o_b = mla_prompt(qn, qpe, lat, kpe, b_w_uk[l], b_w_uv[l]) * silu(cp['b_g'])
        o_c, wkv, last = rwkv_mix(cp['c_x'], jnp.zeros((Bp, C_SHIFT), cp['c_x'].dtype),
                                  jnp.zeros((Bp, H_C, HS_C, HS_C), jnp.float32),
                                  c_mu[l], c_w0[l], c_w_up[l], c_a0[l], c_a_up[l], c_k_k[l], c_k_a[l], c_r_k[l], c_gn[l])
        o_c = o_c * silu(cp['c_g'])
        o_d, _ = gmlp_mix(cp['d_u'], cp['d_v'], d_v_norm[l], d_ws[l], d_b[l], CHUNK_D)
        o_d = o_d * silu(cp['d_g'])
        hp = merge(hp, cp['m_g'], o_a, o_b, o_c, o_d, w_br_a[l], w_br_b[l], w_br_c[l], w_br_d[l], w_out[l])
        pa_k.append(ka); pa_v.append(va); pb_lat.append(lat); pb_kpe.append(kpe)
        pc_wkv.append(wkv); pc_shift.append(last)
        cs = project(hs, ln_g[l], w_in[l])
        qa, ka, va = a_qkv(cs, pos_s, a_q_norm[l], a_k_norm[l])
        o_a = moba_sample(qa, ka, va, cache_a_k[l], cache_a_v[l], page_table) * silu(cs['a_g'])
        qn, qpe, lat, kpe = mla_qkv(cs, pos_s, b_q_a_norm[l], b_w_uq[l], b_q_norm[l], b_kv_a_norm[l], b_kpe_norm[l])
        o_b = mla_sample(qn, qpe, lat, kpe, cache_b_latent[l], cache_b_kpe[l], page_table,
                         b_w_uk[l], b_w_uv[l]) * silu(cs['b_g'])
        o_c, wkv, last = rwkv_mix(cs['c_x'], state_c_shift[l], state_c_wkv[l],
                                  c_mu[l], c_w0[l], c_w_up[l], c_a0[l], c_a_up[l], c_k_k[l], c_k_a[l], c_r_k[l], c_gn[l])
        o_c = o_c * silu(cs['c_g'])
        o_d, v_rows = gmlp_mix(cs['d_u'], cs['d_v'], d_v_norm[l], d_ws[l], d_b[l], T)
        o_d = o_d * silu(cs['d_g'])
        hs = merge(hs, cs['m_g'], o_a, o_b, o_c, o_d, w_br_a[l], w_br_b[l], w_br_c[l], w_br_d[l], w_out[l])
        sa_k.append(ka); sa_v.append(va); sb_lat.append(lat); sb_kpe.append(kpe)
        sc_wkv.append(wkv); sc_shift.append(last); sd_v.append(v_rows)
    st = lambda xs: jnp.stack(xs, axis=0)
    return (hp, hs,
            st(pa_k), st(pa_v), st(pb_lat), st(pb_kpe), st(pc_wkv), st(pc_shift),
            st(sa_k), st(sa_v), st(sb_lat), st(sb_kpe), st(sc_wkv), st(sc_shift), st(sd_v))
```

```python
import functools

import numpy as np
import jax
import jax.numpy as jnp
from jax import lax
from jax.experimental import pallas as pl
from jax.experimental.pallas import tpu as pltpu

F32 = jnp.float32
BF16 = jnp.bfloat16
HIGHEST = lax.Precision.HIGHEST

D_MODEL = 1024
PAGE = 128
ROPE_THETA = 10000.0
NORM_EPS = 1e-6
GN_EPS = 64e-5
HD = 64
H = 4
W_BR = 256
MOBA_BLOCK = 256
MOBA_TOPK = 3
Q_LORA = 256
KV_LORA = 128
NOPE_B = 64
ROPE_B = 32
C_SHIFT = 896
CHUNK_D = 128
CHUNK_C = 64
NEG = -1e30
LANES = 128

P_MG, P_GATES, P_A, P_B, P_C, P_D, P_COLS = 0, 4096, 5120, 5632, 6144, 7168, 7680
O_AQ, O_AK, O_AV, O_AG = 0, 256, 384, 512
O_BDQ, O_BDKV, O_BG = 768, 1024, 1184
O_CX, O_CG = 1440, 2336
O_DU, O_DV, O_DG, O_MG = 2592, 2848, 3104, 3360
A_HEAD_ORDER = (0, 3, 1, 2)

VMEM_LIMIT = 56 * 1024 * 1024


def _cparams(*sem):
    return pltpu.CompilerParams(dimension_semantics=sem, vmem_limit_bytes=VMEM_LIMIT)


def _dot(a, b, precision=None):
    return jnp.dot(a, b, preferred_element_type=F32, precision=precision)


def _dot_nt(a, b, precision=None):
    return lax.dot_general(a, b, (((1,), (1,)), ((), ())), preferred_element_type=F32, precision=precision)


def _dot_tn(a, b, precision=None):
    return lax.dot_general(a, b, (((0,), (0,)), ((), ())), preferred_element_type=F32, precision=precision)


def _seg_sum(x, ones):
    hi = x.astype(BF16)
    lo = (x - hi.astype(F32)).astype(BF16)
    return _dot(hi, ones) + _dot(lo, ones)


def _rope(x, cos, sin_a, sin_b, half):
    return x * cos + pltpu.roll(x, LANES - half, 1) * sin_a + pltpu.roll(x, half, 1) * sin_b


def _sigmoid(x):
    return 1.0 / (1.0 + jnp.exp(-x))


def _silu(x):
    return x * _sigmoid(x)


def _gelu(x):
    return 0.5 * x * (1.0 + jnp.tanh(np.sqrt(2.0 / np.pi).astype(np.float32) * (x + 0.044715 * (x * x * x))))


def _proj_kernel(x_ref, g_ref, w_ref, o_ref, h_ref):
    @pl.when(pl.program_id(1) == 0)
    def _():
        x = x_ref[...]
        y = x * lax.rsqrt(jnp.mean(x * x, axis=-1, keepdims=True) + NORM_EPS)
        h_ref[...] = (y * g_ref[...]).astype(BF16)

    o_ref[...] = _dot(h_ref[...], w_ref[...])


def _proj(x, g, w, tm, tn):
    n, d = x.shape
    c = w.shape[1]
    return pl.pallas_call(
        _proj_kernel,
        grid=(n // tm, c // tn),
        in_specs=[
            pl.BlockSpec((tm, d), lambda i, j: (i, 0)),
            pl.BlockSpec((1, d), lambda i, j: (0, 0)),
            pl.BlockSpec((d, tn), lambda i, j: (0, j)),
        ],
        out_specs=pl.BlockSpec((tm, tn), lambda i, j: (i, j)),
        out_shape=jax.ShapeDtypeStruct((n, c), F32),
        scratch_shapes=[pltpu.VMEM((tm, d), BF16)],
        compiler_params=_cparams("parallel", "arbitrary"),
        name="proj",
    )(x, g.reshape(1, d), w)


def _moba_prep_kernel(a_ref, gq_ref, gk_ref, cos_ref, sa_ref, sb_ref, ones_ref, q_ref, k_ref, kb_ref, vb_ref,
                      *km_ref, blocks):
    a = a_ref[...]
    ones = ones_ref[...]
    cos, sa, sb = cos_ref[...], sa_ref[...], sb_ref[...]
    q, k, v = a[:, :256], a[:, 256:384], a[:, 384:512]
    qn = q * lax.rsqrt(_seg_sum(q * q, ones) * (1.0 / HD) + NORM_EPS) * gq_ref[...]
    kn = k * lax.rsqrt(_seg_sum(k * k, ones[:LANES, :LANES]) * (1.0 / HD) + NORM_EPS) * gk_ref[...]
    q_ref[:, :LANES] = _rope(qn[:, :LANES], cos, sa, sb, HD // 2)
    q_ref[:, LANES:] = _rope(qn[:, LANES:], cos, sa, sb, HD // 2)
    kr = _rope(kn, cos, sa, sb, HD // 2)
    k_ref[...] = kr
    kb_ref[...] = kr.astype(BF16)
    vb_ref[...] = v.astype(BF16)
    if blocks:
        km_ref[0][...] = jnp.mean(kr.reshape(blocks, MOBA_BLOCK, LANES), axis=1)


def _moba_prep(p, gq, gk, tabs, ones, tm, with_kmean):
    n = p.shape[0]
    row = lambda i: (i, 0)
    fix = lambda i: (0, 0)
    blocks = tm // MOBA_BLOCK if with_kmean else 0
    out_shape = [
        jax.ShapeDtypeStruct((n, 256), F32),
        jax.ShapeDtypeStruct((n, LANES), F32),
        jax.ShapeDtypeStruct((n, LANES), BF16),
        jax.ShapeDtypeStruct((n, LANES), BF16),
    ]
    out_specs = [pl.BlockSpec((tm, 256), row), pl.BlockSpec((tm, LANES), row), pl.BlockSpec((tm, LANES), row),
                 pl.BlockSpec((tm, LANES), row)]
    if with_kmean:
        out_shape.append(jax.ShapeDtypeStruct((n // MOBA_BLOCK, LANES), F32))
        out_specs.append(pl.BlockSpec((blocks, LANES), row))
    return pl.pallas_call(
        functools.partial(_moba_prep_kernel, blocks=blocks),
        grid=(n // tm,),
        in_specs=[
            pl.BlockSpec((tm, 512), lambda i: (i, P_A // 512)),
            pl.BlockSpec((1, 256), fix),
            pl.BlockSpec((1, LANES), fix),
            pl.BlockSpec((tm, LANES), row),
            pl.BlockSpec((tm, LANES), row),
            pl.BlockSpec((tm, LANES), row),
            pl.BlockSpec((256, 256), fix),
        ],
        out_specs=out_specs,
        out_shape=out_shape,
        compiler_params=_cparams("parallel"),
        name="moba_prep",
    )(p, gq, gk, *tabs, ones)


def _softmax_first(s, v):
    m = jnp.max(s, axis=-1, keepdims=True)
    p = jnp.exp(s - m)
    return m, jnp.sum(p, axis=-1, keepdims=True), _dot(p.astype(BF16), v)


def _softmax_next(s, v, m, l, acc):
    mn = jnp.maximum(m, jnp.max(s, axis=-1, keepdims=True))
    alpha = jnp.exp(m - mn)
    p = jnp.exp(s - mn)
    return mn, alpha * l + jnp.sum(p, axis=-1, keepdims=True), alpha * acc + _dot(p.astype(BF16), v)


def _moba_attn_kernel(q_ref, k_ref, v_ref, km_ref, o_ref):
    tq = q_ref.shape[0]
    i = pl.program_id(1)
    q = q_ref[...]
    km = km_ref[0]
    lane = lax.broadcasted_iota(jnp.int32, (tq, LANES), 1)
    lower = lane < HD
    row = lax.broadcasted_iota(jnp.int32, (tq, tq), 0)
    col = lax.broadcasted_iota(jnp.int32, (tq, tq), 1)
    erow = lax.broadcasted_iota(jnp.int32, (LANES, tq), 0)
    r0 = pl.multiple_of(i * tq, tq)
    k_own = k_ref[pl.ds(r0, tq), :]
    v_own = v_ref[pl.ds(r0, tq), :]

    consts, init = [], []
    for plane in range(2):
        qp = q[:, plane * LANES:(plane + 1) * LANES]
        for half in range(2):
            qm = jnp.where(lower if half == 0 else jnp.logical_not(lower), qp, 0.0)
            gs = jnp.where(lane < i, _dot_nt(qm, km, HIGHEST), NEG)
            sel = jnp.zeros((tq, LANES), F32)
            for _ in range(MOBA_TOPK):
                m = jnp.max(gs, axis=1, keepdims=True)
                idx = jnp.min(jnp.where(gs == m, lane, LANES), axis=1, keepdims=True)
                pick = jnp.logical_and(lane == idx, m > 0.5 * NEG)
                sel = jnp.where(pick, 1.0, sel)
                gs = jnp.where(pick, NEG, gs)
            qb = (qm * (HD ** -0.5)).astype(BF16)
            s = jnp.where(col <= row, _dot_nt(qb, k_own), NEG)
            init.extend(_softmax_first(s, v_own))
            consts.append((qb, sel.astype(BF16)))

    def body(j, carry):
        c0 = pl.multiple_of(j * tq, tq)
        kj = k_ref[pl.ds(c0, tq), :]
        vj = v_ref[pl.ds(c0, tq), :]
        ej = jnp.where(erow == j, 1.0, 0.0).astype(BF16)
        out = []
        for hh, (qb, selb) in enumerate(consts):
            picked = _dot(selb, ej)
            s = jnp.where(picked > 0.5, _dot_nt(qb, kj), NEG)
            out.extend(_softmax_next(s, vj, *carry[3 * hh:3 * hh + 3]))
        return tuple(out)

    fin = lax.fori_loop(0, i, body, tuple(init))
    for plane in range(2):
        o0 = fin[6 * plane + 2] / fin[6 * plane + 1]
        o1 = fin[6 * plane + 5] / fin[6 * plane + 4]
        o_ref[:, plane * LANES:(plane + 1) * LANES] = jnp.where(lower, o0, o1)


def _moba_attn(q, kb, vb, km, batch, seq):
    nq = seq // MOBA_BLOCK
    return pl.pallas_call(
        _moba_attn_kernel,
        grid=(batch, nq),
        in_specs=[
            pl.BlockSpec((MOBA_BLOCK, 256), lambda b, i: (b * nq + i, 0)),
            pl.BlockSpec((seq, LANES), lambda b, i: (b, 0)),
            pl.BlockSpec((seq, LANES), lambda b, i: (b, 0)),
            pl.BlockSpec((1, LANES, LANES), lambda b, i: (b, 0, 0)),
        ],
        out_specs=pl.BlockSpec((MOBA_BLOCK, 256), lambda b, i: (b * nq + i, 0)),
        out_shape=jax.ShapeDtypeStruct((batch * seq, 256), F32),
        compiler_params=_cparams("parallel", "arbitrary"),
        name="moba_attn",
    )(q, kb, vb, km)


def _mla_prep_kernel(b_ref, gqa_ref, wuq_ref, gq_ref, gkv_ref, gkpe_ref, wuk_ref, wuv_ref, cos_ref, sa_ref, sb_ref,
                     *out_refs, decode):
    x = b_ref[...]
    cos, sa, sb = cos_ref[...], sa_ref[...], sb_ref[...]
    dq, lat_raw, kpe_raw = x[:, :256], x[:, 256:384], x[:, 384:512]
    cq = dq * lax.rsqrt(jnp.mean(dq * dq, axis=-1, keepdims=True) + NORM_EPS) * gqa_ref[...]
    q = _dot(cq.astype(BF16), wuq_ref[...])
    lat = lat_raw * lax.rsqrt(jnp.mean(lat_raw * lat_raw, axis=-1, keepdims=True) + NORM_EPS) * gkv_ref[...]
    kpe = kpe_raw * lax.rsqrt(jnp.sum(kpe_raw * kpe_raw, axis=-1, keepdims=True) * (1.0 / ROPE_B) + NORM_EPS)
    kpe = _rope(kpe * gkpe_ref[...], cos, sa, sb, ROPE_B // 2)
    scale = (NOPE_B + ROPE_B) ** -0.5
    qs = []
    for h in range(H):
        qh = q[:, h * LANES:(h + 1) * LANES]
        qh = qh * lax.rsqrt(jnp.sum(qh * qh, axis=-1, keepdims=True) * (1.0 / (NOPE_B + ROPE_B)) + NORM_EPS)
        qs.append(_rope(qh * gq_ref[...], cos, sa, sb, ROPE_B // 2) * scale)
    latb = lat.astype(BF16)
    if decode:
        qlat_ref, qpe_ref, snew_ref, lat_ref, kpe_ref = out_refs
        lane = lax.broadcasted_iota(jnp.int32, kpe.shape, 1)
        snew = jnp.zeros(kpe.shape, F32)
        for h in range(H):
            ql = _dot(qs[h].astype(BF16), wuk_ref[h])
            qlat_ref[:, h * LANES:(h + 1) * LANES] = ql
            qpe_ref[:, h * LANES:(h + 1) * LANES] = qs[h]
            s = jnp.sum(ql.astype(BF16).astype(F32) * latb.astype(F32), axis=-1, keepdims=True)
            s = s + jnp.sum(qs[h].astype(BF16).astype(F32) * kpe.astype(BF16).astype(F32), axis=-1, keepdims=True)
            snew = jnp.where(lane == h, s, snew)
        snew_ref[...] = snew
    else:
        qb_ref, kb_ref, vb_ref, lat_ref, kpe_ref = out_refs
        kn = _dot(latb, wuk_ref[...])
        for h in range(H):
            qb_ref[:, h * LANES:(h + 1) * LANES] = qs[h].astype(BF16)
            kb_ref[:, h * LANES:(h + 1) * LANES] = (kn[:, h * LANES:(h + 1) * LANES] + kpe).astype(BF16)
        vb_ref[...] = _dot(latb, wuv_ref[...]).astype(BF16)
    lat_ref[...] = lat
    kpe_ref[...] = kpe


def _mla_prep(p, gqa, wuq, gq, gkv, gkpe, wuk, wuv, tabs, tm, decode):
    n = p.shape[0]
    row = lambda i: (i, 0)
    fix = lambda i: (0, 0)
    wide = lambda dt: (jax.ShapeDtypeStruct((n, 512), dt), pl.BlockSpec((tm, 512), row))
    plane = lambda dt: (jax.ShapeDtypeStruct((n, LANES), dt), pl.BlockSpec((tm, LANES), row))
    if decode:
        outs = [wide(F32), wide(F32), plane(F32), plane(F32), plane(F32)]
        wuk_spec = pl.BlockSpec((H, LANES, LANES), lambda i: (0, 0, 0))
    else:
        outs = [wide(BF16), wide(BF16), (jax.ShapeDtypeStruct((n, 256), BF16), pl.BlockSpec((tm, 256), row)),
                plane(F32), plane(F32)]
        wuk_spec = pl.BlockSpec((LANES, 512), fix)
    return pl.pallas_call(
        functools.partial(_mla_prep_kernel, decode=decode),
        grid=(n // tm,),
        in_specs=[
            pl.BlockSpec((tm, 512), lambda i: (i, P_B // 512)),
            pl.BlockSpec((1, 256), fix),
            pl.BlockSpec((256, 512), fix),
            pl.BlockSpec((1, LANES), fix),
            pl.BlockSpec((1, LANES), fix),
            pl.BlockSpec((1, LANES), fix),
            wuk_spec,
            pl.BlockSpec((LANES, 256), fix),
            pl.BlockSpec((tm, LANES), row),
            pl.BlockSpec((tm, LANES), row),
            pl.BlockSpec((tm, LANES), row),
        ],
        out_specs=[s for _, s in outs],
        out_shape=[s for s, _ in outs],
        compiler_params=_cparams("parallel"),
        name="mla_prep",
    )(p, gqa, wuq, gq, gkv, gkpe, wuk, wuv, *tabs)


def _mla_attn_kernel(q_ref, k_ref, v_ref, o_ref):
    tq = q_ref.shape[0]
    i = pl.program_id(1)
    lane = lax.broadcasted_iota(jnp.int32, (tq, LANES), 1)
    lower = lane < HD
    row = lax.broadcasted_iota(jnp.int32, (tq, tq), 0)
    col = lax.broadcasted_iota(jnp.int32, (tq, tq), 1)
    r0 = pl.multiple_of(i * tq, tq)
    qs = [q_ref[:, h * LANES:(h + 1) * LANES] for h in range(H)]

    def tile(c0, h):
        return (k_ref[pl.ds(c0, tq), h * LANES:(h + 1) * LANES],
                v_ref[pl.ds(c0, tq), (h // 2) * LANES:(h // 2 + 1) * LANES])

    init = []
    for h in range(H):
        kh, vh = tile(r0, h)
        init.extend(_softmax_first(jnp.where(col <= row, _dot_nt(qs[h], kh), NEG), vh))

    def body(j, carry):
        c0 = pl.multiple_of(j * tq, tq)
        out = []
        for h in range(H):
            kh, vh = tile(c0, h)
            out.extend(_softmax_next(_dot_nt(qs[h], kh), vh, *carry[3 * h:3 * h + 3]))
        return tuple(out)

    fin = lax.fori_loop(0, i, body, tuple(init))
    for plane in range(2):
        o0 = fin[6 * plane + 2] / fin[6 * plane + 1]
        o1 = fin[6 * plane + 5] / fin[6 * plane + 4]
        o_ref[:, plane * LANES:(plane + 1) * LANES] = jnp.where(lower, o0, o1)


def _mla_attn(qb, kb, vb, batch, seq, tq):
    nq = seq // tq
    return pl.pallas_call(
        _mla_attn_kernel,
        grid=(batch, nq),
        in_specs=[
            pl.BlockSpec((tq, 512), lambda b, i: (b * nq + i, 0)),
            pl.BlockSpec((seq, 512), lambda b, i: (b, 0)),
            pl.BlockSpec((seq, 256), lambda b, i: (b, 0)),
        ],
        out_specs=pl.BlockSpec((tq, 256), lambda b, i: (b * nq + i, 0)),
        out_shape=jax.ShapeDtypeStruct((batch * seq, 256), F32),
        compiler_params=_cparams("parallel", "arbitrary"),
        name="mla_attn",
    )(qb, kb, vb)


def _rwkv_features(xc, prev, mu, w0, wup, a0, aup, kk_w, ka_w, ones):
    xs = xc + (prev - xc) * mu
    r, k, v, lora = xs[:, :256], xs[:, 256:512], xs[:, 512:768], xs[:, 768:896]
    w_pre = w0 + _dot(jnp.tanh(lora).astype(BF16), wup)
    z = -w_pre
    w_log = -(jnp.maximum(z, 0.0) + jnp.log(1.0 + jnp.exp(-jnp.abs(z)))) - 0.5
    lw = -jnp.exp(w_log)
    a = _sigmoid(a0 + _dot(lora.astype(BF16), aup))
    kk = k * kk_w
    kk = kk * lax.rsqrt(_seg_sum(kk * kk, ones) + 1e-12)
    k2 = k * (1.0 + (a - 1.0) * ka_w)
    return r, k2, v, kk, kk * a, lw


def _rwkv_out(y, r, k2, v, rk_w, gn_w, ones):
    mean = _seg_sum(y, ones) * (1.0 / HD)
    d = y - mean
    var = _seg_sum(d * d, ones) * (1.0 / HD)
    return d * lax.rsqrt(var + GN_EPS) * gn_w + _seg_sum(r * k2 * rk_w, ones) * v


def _block_stack(x, same_blk):
    return jnp.where(same_blk, jnp.concatenate([x, x, x, x], axis=0), 0.0)


def _row_blocks_sum(x):
    c = CHUNK_C
    return x[0:c] + x[c:2 * c] + x[2 * c:3 * c] + x[3 * c:4 * c]


def _rwkv_seq_kernel(x_ref, mu_ref, w0_ref, wup_ref, a0_ref, aup_ref, kk_ref, ka_ref, rk_ref, gn_ref, ones_ref,
                     o_ref, s_out_ref, last_ref, s_ref, prev_ref):
    tt = x_ref.shape[0]
    c = CHUNK_C

    @pl.when(pl.program_id(1) == 0)
    def _():
        s_ref[...] = jnp.zeros(s_ref.shape, F32)
        prev_ref[...] = jnp.zeros(prev_ref.shape, F32)

    ones = ones_ref[...]
    xc = x_ref[...]
    rid = lax.broadcasted_iota(jnp.int32, xc.shape, 0)
    prev = jnp.where(rid == 0, prev_ref[...], pltpu.roll(xc, 1, 0))
    prev_ref[...] = xc[tt - 1:tt, :]
    last_ref[0] = xc[tt - 1:tt, :]
    r, k2, v, kk, bv, lw = _rwkv_features(xc, prev, mu_ref[...], w0_ref[...], wup_ref[...], a0_ref[...], aup_ref[...],
                                          kk_ref[...], ka_ref[...], ones)

    rr = lax.broadcasted_iota(jnp.int32, (256, 256), 0)
    cc = lax.broadcasted_iota(jnp.int32, (256, 256), 1)
    same_blk = (rr >> 6) == (cc >> 6)
    strict = (rr & 63) > (cc & 63)
    incl = (rr & 63) >= (cc & 63)
    eye = jnp.where(rr == cc, 1.0, 0.0)
    tri = jnp.where(incl[:c, :c], 1.0, 0.0)

    s = s_ref[...]
    ys = []
    for ci in range(tt // c):
        sl = slice(ci * c, (ci + 1) * c)
        lw_c, r_c, k_c, v_c, kk_c, bv_c = lw[sl], r[sl], k2[sl], v[sl], kk[sl], bv[sl]
        cum = _dot(tri, lw_c, HIGHEST)
        last = cum[c - 1:c, :]
        e_neg = jnp.exp(-cum)
        e_end = jnp.exp(last - cum)
        a_bd = _block_stack(-kk_c * jnp.exp(cum - lw_c), same_blk)
        r_bd = _block_stack(r_c * jnp.exp(cum), same_blk)
        b_bd = _block_stack(bv_c * e_neg, same_blk)
        k_bd = _block_stack(k_c * e_neg, same_blk)
        v_bd = _block_stack(v_c, same_blk)
        n_ab = jnp.where(strict, _dot_nt(a_bd, b_bd), 0.0)
        n_ak = jnp.where(strict, _dot_nt(a_bd, k_bd), 0.0)
        n_rb = jnp.where(incl, _dot_nt(r_bd, b_bd), 0.0)
        n_rk = jnp.where(incl, _dot_nt(r_bd, k_bd), 0.0)
        t_inv = eye + n_ab
        pw = n_ab
        for _ in range(5):
            pw = _dot(pw, pw)
            t_inv = t_inv + _dot(t_inv, pw)
        u_bd = _dot(t_inv, _dot_nt(a_bd, s) + _dot(n_ak, v_bd))
        y_bd = _dot_nt(r_bd, s) + _dot(n_rb, u_bd) + _dot(n_rk, v_bd)
        ys.append(_row_blocks_sum(y_bd))
        uv = jnp.concatenate([_row_blocks_sum(u_bd), v_c], axis=0)
        bk = jnp.concatenate([bv_c * e_end, k_c * e_end], axis=0)
        s = s * jnp.exp(last) + jnp.where(same_blk, _dot_tn(uv, bk), 0.0)
    s_ref[...] = s
    s_out_ref[0] = s
    y = jnp.concatenate(ys, axis=0)
    o_ref[...] = _rwkv_out(y, r, k2, v, rk_ref[...], gn_ref[...], ones)


def _rwkv_seq(p, prm, ones, batch, seq, tt):
    nt = seq // tt
    fix = lambda b, t: (0, 0)
    vec256 = pl.BlockSpec((1, 256), fix)
    return pl.pallas_call(
        _rwkv_seq_kernel,
        grid=(batch, nt),
        in_specs=[
            pl.BlockSpec((tt, 1024), lambda b, t: (b * nt + t, P_C // 1024)),
            pl.BlockSpec((1, 1024), fix),
            vec256, pl.BlockSpec((LANES, 256), fix), vec256, pl.BlockSpec((LANES, 256), fix),
            vec256, vec256, vec256, vec256,
            pl.BlockSpec((256, 256), fix),
        ],
        out_specs=[
            pl.BlockSpec((tt, 256), lambda b, t: (b * nt + t, 0)),
            pl.BlockSpec((1, 256, 256), lambda b, t: (b, 0, 0)),
            pl.BlockSpec((1, 1, 1024), lambda b, t: (b, 0, 0)),
        ],
        out_shape=[
            jax.ShapeDtypeStruct((batch * seq, 256), F32),
            jax.ShapeDtypeStruct((batch, 256, 256), F32),
            jax.ShapeDtypeStruct((batch, 1, 1024), F32),
        ],
        scratch_shapes=[pltpu.VMEM((256, 256), F32), pltpu.VMEM((1, 1024), F32)],
        compiler_params=_cparams("parallel", "arbitrary"),
        name="rwkv_seq",
    )(p, prm["mu"], prm["w0"], prm["wup"], prm["a0"], prm["aup"], prm["kk"], prm["ka"], prm["rk"], prm["gn"], ones)


def _rwkv_step_kernel(x_ref, prev_ref, s_ref, mu_ref, w0_ref, wup_ref, a0_ref, aup_ref, kk_ref, ka_ref, rk_ref,
                      gn_ref, ones_ref, o_ref, s_out_ref, yt_ref):
    nb = x_ref.shape[0]
    ones = ones_ref[...]
    r, k2, v, kk, bv, lw = _rwkv_features(x_ref[...], prev_ref[...], mu_ref[...], w0_ref[...], wup_ref[...],
                                          a0_ref[...], aup_ref[...], kk_ref[...], ka_ref[...], ones)
    w = jnp.exp(lw)
    vt = v.T
    for b in range(nb):
        for h in range(H):
            hs = slice(h * HD, (h + 1) * HD)
            st = s_ref[b, h]
            sa = -jnp.sum(st * kk[b:b + 1, hs], axis=-1, keepdims=True)
            st = st * w[b:b + 1, hs] + sa * bv[b:b + 1, hs] + vt[hs, b:b + 1] * k2[b:b + 1, hs]
            s_out_ref[b, h] = st
            yt_ref[hs, b:b + 1] = jnp.sum(st * r[b:b + 1, hs], axis=-1, keepdims=True)
    y = yt_ref[...].T
    o_ref[...] = _rwkv_out(y, r, k2, v, rk_ref[...], gn_ref[...], ones)


def _rwkv_step(p, prev, state, prm, ones, nb):
    n = p.shape[0]
    fix = lambda i: (0, 0)
    vec256 = pl.BlockSpec((1, 256), fix)
    return pl.pallas_call(
        _rwkv_step_kernel,
        grid=(n // nb,),
        in_specs=[
            pl.BlockSpec((nb, 1024), lambda i: (i, P_C // 1024)),
            pl.BlockSpec((nb, 1024), lambda i: (i, 0)),
            pl.BlockSpec((nb, H, HD, HD), lambda i: (i, 0, 0, 0)),
            pl.BlockSpec((1, 1024), fix),
            vec256, pl.BlockSpec((LANES, 256), fix), vec256, pl.BlockSpec((LANES, 256), fix),
            vec256, vec256, vec256, vec256,
            pl.BlockSpec((256, 256), fix),
        ],
        out_specs=[
            pl.BlockSpec((nb, 256), lambda i: (i, 0)),
            pl.BlockSpec((nb, H, HD, HD), lambda i: (i, 0, 0, 0)),
        ],
        out_shape=[
            jax.ShapeDtypeStruct((n, 256), F32),
            jax.ShapeDtypeStruct((n, H, HD, HD), F32),
        ],
        scratch_shapes=[pltpu.VMEM((256, nb), F32)],
        compiler_params=_cparams("parallel"),
        name="rwkv_step",
    )(p, prev, state, prm["mu"], prm["w0"], prm["wup"], prm["a0"], prm["aup"], prm["kk"], prm["ka"], prm["rk"],
      prm["gn"], ones)


def _merge_kernel(x_ref, mg_ref, g_ref, oa_ref, ob_ref, oc_ref, d_ref, vn_ref, ws_ref, bias_ref, wbr_ref, wout_ref,
                  o_ref, *v_ref, chunk):
    tm = x_ref.shape[0]
    d = d_ref[...]
    u = _gelu(d[:, :256])
    vg = _gelu(d[:, 256:512])
    v = vg * lax.rsqrt(jnp.mean(vg * vg, axis=-1, keepdims=True) + NORM_EPS) * vn_ref[...]
    if chunk == 1:
        sp = v * ws_ref[...] + bias_ref[...]
        v_ref[0][...] = v
    else:
        rr = lax.broadcasted_iota(jnp.int32, (chunk, chunk), 0)
        cc = lax.broadcasted_iota(jnp.int32, (chunk, chunk), 1)
        grp = lax.broadcasted_iota(jnp.int32, (chunk, 256), 1) >> 6
        vb = v.astype(BF16)
        ws = [jnp.where(cc <= rr, ws_ref[g], 0.0).astype(BF16) for g in range(H)]
        parts = []
        for ci in range(tm // chunk):
            vc = vb[ci * chunk:(ci + 1) * chunk]
            acc = bias_ref[...]
            for g in range(H):
                acc = acc + jnp.where(grp == g, _dot(ws[g], vc), 0.0)
            parts.append(acc)
        sp = jnp.concatenate(parts, axis=0)
    od = u * sp
    gates = g_ref[...]
    y = jnp.zeros((tm, D_MODEL), F32)
    for bi, o in enumerate((oa_ref[...], ob_ref[...], oc_ref[...], od)):
        z = (o * _silu(gates[:, bi * 256:(bi + 1) * 256])).astype(BF16)
        y = y + _sigmoid(mg_ref[:, bi * D_MODEL:(bi + 1) * D_MODEL]) * _dot(z, wbr_ref[bi])
    o_ref[...] = x_ref[...] + _dot(y.astype(BF16), wout_ref[...])


def _merge(x, p, oa, ob, oc, vn, ws, bias, wbr, wout, tm, chunk):
    n = x.shape[0]
    row = lambda i: (i, 0)
    fix = lambda i: (0, 0)
    out_shape = [jax.ShapeDtypeStruct((n, D_MODEL), F32)]
    out_specs = [pl.BlockSpec((tm, D_MODEL), row)]
    if chunk == 1:
        out_shape.append(jax.ShapeDtypeStruct((n, 256), F32))
        out_specs.append(pl.BlockSpec((tm, 256), row))
        ws_spec = pl.BlockSpec((1, 256), fix)
        bias_spec = pl.BlockSpec((1, 256), fix)
    else:
        ws_spec = pl.BlockSpec((H, chunk, chunk), lambda i: (0, 0, 0))
        bias_spec = pl.BlockSpec((chunk, 256), fix)
    return pl.pallas_call(
        functools.partial(_merge_kernel, chunk=chunk),
        grid=(n // tm,),
        in_specs=[
            pl.BlockSpec((tm, D_MODEL), row),
            pl.BlockSpec((tm, 4096), lambda i: (i, P_MG // 4096)),
            pl.BlockSpec((tm, 1024), lambda i: (i, P_GATES // 1024)),
            pl.BlockSpec((tm, 256), row),
            pl.BlockSpec((tm, 256), row),
            pl.BlockSpec((tm, 256), row),
            pl.BlockSpec((tm, 512), lambda i: (i, P_D // 512)),
            pl.BlockSpec((1, 256), fix),
            ws_spec,
            bias_spec,
            pl.BlockSpec((H, 256, D_MODEL), lambda i: (0, 0, 0)),
            pl.BlockSpec((D_MODEL, D_MODEL), fix),
        ],
        out_specs=out_specs,
        out_shape=out_shape,
        compiler_params=_cparams("parallel"),
        name="merge",
    )(x, p, p, oa, ob, oc, p, vn, ws, bias, wbr, wout)


def _moba_pick_kernel(pt_ref, q_ref, *refs, pages_per_step):
    page_refs, idx_ref, km_ref = refs[:pages_per_step], refs[pages_per_step], refs[pages_per_step + 1]
    j = pl.program_id(1)
    blocks = pages_per_step * PAGE // MOBA_BLOCK
    per_blk = MOBA_BLOCK // PAGE
    rows = []
    for bi in range(blocks):
        acc = jnp.sum(page_refs[bi * per_blk][0], axis=0, keepdims=True)
        for pi in range(1, per_blk):
            acc = acc + jnp.sum(page_refs[bi * per_blk + pi][0], axis=0, keepdims=True)
        rows.append(acc * (1.0 / MOBA_BLOCK))
    km_ref[pl.ds(pl.multiple_of(j * blocks, blocks), blocks), :] = jnp.concatenate(rows, axis=0)

    @pl.when(j == pl.num_programs(1) - 1)
    def _():
        nblk = km_ref.shape[0]
        q = q_ref[0]
        lane = lax.broadcasted_iota(jnp.int32, (8, nblk), 1)
        out_lane = lax.broadcasted_iota(jnp.int32, (8, LANES), 1)
        gs = _dot_nt(q, km_ref[...], HIGHEST)
        out = jnp.zeros((8, LANES), jnp.int32)
        for t in range(MOBA_TOPK):
            m = jnp.max(gs, axis=1, keepdims=True)
            idx = jnp.min(jnp.where(gs == m, lane, nblk), axis=1, keepdims=True)
            out = jnp.where(out_lane == t, idx, out)
            gs = jnp.where(lane == idx, NEG, gs)
        idx_ref[0] = out


def _moba_pick(page_table, q8, cache_k, layer_base, pages_per_step):
    nb, n_pages = page_table.shape
    n_full = n_pages * PAGE // MOBA_BLOCK
    steps = n_pages // pages_per_step

    def page_map(g):
        return lambda b, j, pt: (layer_base + pt[b, j * pages_per_step + g], 0, 0)

    grid_spec = pltpu.PrefetchScalarGridSpec(
        num_scalar_prefetch=1,
        grid=(nb, steps),
        in_specs=[pl.BlockSpec((1, 8, LANES), lambda b, j, pt: (b, 0, 0))]
        + [pl.BlockSpec((1, PAGE, LANES), page_map(g)) for g in range(pages_per_step)],
        out_specs=pl.BlockSpec((1, 8, LANES), lambda b, j, pt: (b, 0, 0)),
        scratch_shapes=[pltpu.VMEM((n_full, LANES), F32)],
    )
    return pl.pallas_call(
        functools.partial(_moba_pick_kernel, pages_per_step=pages_per_step),
        grid_spec=grid_spec,
        out_shape=jax.ShapeDtypeStruct((nb, 8, LANES), jnp.int32),
        compiler_params=_cparams("parallel", "arbitrary"),
        name="moba_pick",
    )(page_table, q8, *([cache_k] * pages_per_step))


def _moba_decode_kernel(idx_ref, pt_ref, q_ref, kn_ref, vn_ref, *refs):
    n_sel = H * MOBA_TOPK * (MOBA_BLOCK // PAGE)
    k_refs, v_refs, o_ref = refs[:n_sel], refs[n_sel:2 * n_sel], refs[2 * n_sel]
    per_head = n_sel // H
    q = q_ref[0]
    kn = kn_ref[0]
    vn = vn_ref[0]
    rowid = lax.broadcasted_iota(jnp.int32, (8, LANES), 0)
    out = jnp.zeros((8, LANES), F32)
    for h in range(H):
        qh = jnp.where(rowid == h, q, 0.0)
        qb = (qh * (HD ** -0.5)).astype(BF16)
        s_own = jnp.sum(qb.astype(F32) * kn.astype(BF16).astype(F32), axis=-1, keepdims=True)
        ss = [_dot_nt(qb, k_refs[h * per_head + t][0].astype(BF16)) for t in range(per_head)]
        m = s_own
        for s in ss:
            m = jnp.maximum(m, jnp.max(s, axis=-1, keepdims=True))
        p_own = jnp.exp(s_own - m)
        l = p_own
        acc = p_own.astype(BF16).astype(F32) * vn.astype(BF16).astype(F32)
        for t, s in enumerate(ss):
            p = jnp.exp(s - m)
            l = l + jnp.sum(p, axis=-1, keepdims=True)
            acc = acc + _dot(p.astype(BF16), v_refs[h * per_head + t][0].astype(BF16))
        out = jnp.where(rowid == h, acc / l, out)
    o_ref[0] = out


def _moba_decode(idx, page_table, q8, k_new, v_new, cache_k, cache_v, layer_base):
    nb = page_table.shape[0]
    per_blk = MOBA_BLOCK // PAGE

    def page_map(h, t, pi):
        return lambda b, idx_r, pt: (layer_base + pt[b, idx_r[b, h * MOBA_TOPK + t] * per_blk + pi], 0, 0)

    sel = [(h, t, pi) for h in range(H) for t in range(MOBA_TOPK) for pi in range(per_blk)]
    page_specs = [pl.BlockSpec((1, PAGE, LANES), page_map(*s)) for s in sel]
    vec = pl.BlockSpec((1, 1, LANES), lambda b, idx_r, pt: (b, 0, 0))
    grid_spec = pltpu.PrefetchScalarGridSpec(
        num_scalar_prefetch=2,
        grid=(nb,),
        in_specs=[pl.BlockSpec((1, 8, LANES), lambda b, idx_r, pt: (b, 0, 0)), vec, vec] + page_specs + page_specs,
        out_specs=pl.BlockSpec((1, 8, LANES), lambda b, idx_r, pt: (b, 0, 0)),
    )
    return pl.pallas_call(
        _moba_decode_kernel,
        grid_spec=grid_spec,
        out_shape=jax.ShapeDtypeStruct((nb, 8, LANES), F32),
        compiler_params=_cparams("parallel"),
        name="moba_decode",
    )(idx, page_table, q8, k_new.reshape(nb, 1, LANES), v_new.reshape(nb, 1, LANES),
      *([cache_k] * len(sel)), *([cache_v] * len(sel)))


def _mla_decode_kernel(pt_ref, ql_ref, qp_ref, sn_ref, ln_ref, *refs, pages_per_step):
    lat_refs, kpe_refs = refs[:pages_per_step], refs[pages_per_step:2 * pages_per_step]
    o_ref, m_ref, l_ref, acc_ref = refs[2 * pages_per_step:]
    j = pl.program_id(1)

    @pl.when(j == 0)
    def _():
        m_ref[...] = sn_ref[0]
        l_ref[...] = jnp.ones(l_ref.shape, F32)
        acc_ref[...] = jnp.broadcast_to(ln_ref[0].astype(BF16).astype(F32), acc_ref.shape)

    ql = ql_ref[0].astype(BF16)
    qp = qp_ref[0].astype(BF16)
    lats = [r[0].astype(BF16) for r in lat_refs]
    ss = [_dot_nt(ql, lats[g]) + _dot_nt(qp, kpe_refs[g][0].astype(BF16)) for g in range(pages_per_step)]
    s = jnp.concatenate(ss, axis=1)
    m = m_ref[...]
    mn = jnp.maximum(m, jnp.max(s, axis=-1, keepdims=True))
    alpha = jnp.exp(m - mn)
    p = jnp.exp(s - mn).astype(BF16)
    l_ref[...] = alpha * l_ref[...] + jnp.sum(p.astype(F32), axis=-1, keepdims=True)
    acc = alpha * acc_ref[...]
    for g in range(pages_per_step):
        acc = acc + _dot(p[:, g * PAGE:(g + 1) * PAGE], lats[g])
    acc_ref[...] = acc
    m_ref[...] = mn

    @pl.when(j == pl.num_programs(1) - 1)
    def _():
        o_ref[0] = acc_ref[...] / l_ref[...]


def _mla_decode(page_table, qlat8, qpe8, snew8, lat_new, cache_lat, cache_kpe, layer_base, pages_per_step):
    nb, n_pages = page_table.shape
    steps = n_pages // pages_per_step

    def page_map(g):
        return lambda b, j, pt: (layer_base + pt[b, j * pages_per_step + g], 0, 0)

    per_b = lambda shape: pl.BlockSpec((1,) + shape, lambda b, j, pt: (b, 0, 0))
    grid_spec = pltpu.PrefetchScalarGridSpec(
        num_scalar_prefetch=1,
        grid=(nb, steps),
        in_specs=[per_b((8, LANES)), per_b((8, ROPE_B)), per_b((8, 1)), per_b((1, LANES))]
        + [pl.BlockSpec((1, PAGE, KV_LORA), page_map(g)) for g in range(pages_per_step)]
        + [pl.BlockSpec((1, PAGE, ROPE_B), page_map(g)) for g in range(pages_per_step)],
        out_specs=per_b((8, LANES)),
        scratch_shapes=[pltpu.VMEM((8, 1), F32), pltpu.VMEM((8, 1), F32), pltpu.VMEM((8, LANES), F32)],
    )
    return pl.pallas_call(
        functools.partial(_mla_decode_kernel, pages_per_step=pages_per_step),
        grid_spec=grid_spec,
        out_shape=jax.ShapeDtypeStruct((nb, 8, LANES), F32),
        compiler_params=_cparams("parallel", "arbitrary"),
        name="mla_decode",
    )(page_table, qlat8, qpe8, snew8, lat_new.reshape(nb, 1, LANES),
      *([cache_lat] * pages_per_step), *([cache_kpe] * pages_per_step))


def _mla_up_kernel(o_ref, w_ref, out_ref):
    acc = jnp.zeros(out_ref.shape, F32)
    for h in range(H):
        acc = acc + _dot(o_ref[:, h * LANES:(h + 1) * LANES].astype(BF16), w_ref[h])
    out_ref[...] = acc


def _mla_up(o_lat, wuv_heads):
    n = o_lat.shape[0]
    return pl.pallas_call(
        _mla_up_kernel,
        out_shape=jax.ShapeDtypeStruct((n, 256), F32),
        name="mla_up",
    )(o_lat, wuv_heads)


def _pack_w_in(w):
    d = w.shape[0]
    z = lambda n: jnp.zeros((d, n), w.dtype)
    col = lambda o, n: w[:, o:o + n]
    aq = jnp.concatenate([col(O_AQ + h * HD, HD) for h in A_HEAD_ORDER], axis=1)
    ag = jnp.concatenate([col(O_AG + h * HD, HD) for h in A_HEAD_ORDER], axis=1)
    parts = [
        col(O_MG, 4096),
        ag, col(O_BG, 256), col(O_CG, 256), col(O_DG, 256),
        aq, col(O_AK, 128), col(O_AV, 128),
        col(O_BDQ, 256), col(O_BDKV, KV_LORA), z(64), col(O_BDKV + KV_LORA, ROPE_B), z(32),
        col(O_CX, C_SHIFT), z(128),
        col(O_DU, 256), col(O_DV, 256),
    ]
    out = jnp.concatenate(parts, axis=1).astype(BF16)
    assert out.shape[1] == P_COLS
    return out


def _rope_tables(pos, lane_lo, dim):
    half = dim // 2
    inv = ROPE_THETA ** (-jnp.arange(half, dtype=F32) / half)
    ang = pos.astype(F32)[:, None] * inv[None, :]
    cos, sin = jnp.cos(ang), jnp.sin(ang)
    lane = np.arange(LANES)
    span = LANES if dim == HD else dim
    active = (lane >= lane_lo) & (lane < lane_lo + span)
    rel = (lane - lane_lo) % dim
    first = rel < half
    gather = rel % half
    cos_t = jnp.where(active[None, :], cos[:, gather], 1.0)
    sin_t = sin[:, gather]
    sin_a = jnp.where((active & first)[None, :], -sin_t, 0.0)
    sin_b = jnp.where((active & ~first)[None, :], sin_t, 0.0)
    return cos_t, sin_a, sin_b


def _lane_pad(x, lo, width=LANES):
    out = jnp.zeros(x.shape[:-1] + (width,), x.dtype)
    return out.at[..., lo:lo + x.shape[-1]].set(x)


def _layer_params(l, a_q_norm, a_k_norm, b_q_a_norm, b_w_uq, b_q_norm, b_kv_a_norm, b_kpe_norm, b_w_uk, b_w_uv,
                  c_mu, c_w0, c_w_up, c_a0, c_a_up, c_k_k, c_k_a, c_r_k, c_gn, d_v_norm, d_ws, d_b,
                  w_br_a, w_br_b, w_br_c, w_br_d, w_out):
    prm = {}
    prm["gq"] = jnp.tile(a_q_norm[l], 4).reshape(1, 256)
    prm["gk"] = jnp.tile(a_k_norm[l], 2).reshape(1, LANES)
    prm["gqa"] = b_q_a_norm[l].reshape(1, Q_LORA)
    prm["wuq"] = _lane_pad(b_w_uq[l], 0).reshape(Q_LORA, H * LANES).astype(BF16)
    prm["gqb"] = _lane_pad(b_q_norm[l], 0).reshape(1, LANES)
    prm["gkv"] = b_kv_a_norm[l].reshape(1, KV_LORA)
    prm["gkpe"] = _lane_pad(b_kpe_norm[l], 64).reshape(1, LANES)
    prm["wuk"] = _lane_pad(b_w_uk[l], 0).reshape(KV_LORA, H * LANES).astype(BF16)
    wukt = jnp.transpose(b_w_uk[l], (1, 2, 0))
    prm["wuk_t"] = jnp.concatenate([wukt, jnp.zeros((H, LANES - NOPE_B, KV_LORA), F32)], axis=1).astype(BF16)
    prm["wuv"] = b_w_uv[l].reshape(KV_LORA, H * HD).astype(BF16)
    wuv_h = jnp.zeros((H, KV_LORA, 256), F32)
    for h in range(H):
        wuv_h = wuv_h.at[h, :, h * HD:(h + 1) * HD].set(b_w_uv[l][:, h, :])
    prm["wuv_heads"] = wuv_h.astype(BF16)
    rw = {}
    rw["mu"] = _lane_pad(c_mu[l], 0, 1024).reshape(1, 1024)
    rw["w0"] = c_w0[l].reshape(1, 256)
    rw["wup"] = jnp.concatenate([c_w_up[l], jnp.zeros((64, 256), F32)], axis=0).astype(BF16)
    rw["a0"] = c_a0[l].reshape(1, 256)
    rw["aup"] = jnp.concatenate([jnp.zeros((64, 256), F32), c_a_up[l]], axis=0).astype(BF16)
    rw["kk"] = c_k_k[l].reshape(1, 256)
    rw["ka"] = c_k_a[l].reshape(1, 256)
    rw["rk"] = c_r_k[l].reshape(1, 256)
    rw["gn"] = c_gn[l].reshape(1, 256)
    prm["rwkv"] = rw
    prm["vn"] = d_v_norm[l].reshape(1, 256)
    prm["ws"] = d_ws[l]
    prm["bias"] = jnp.repeat(d_b[l].T, HD, axis=1)
    prm["ws1"] = jnp.repeat(d_ws[l][:, 0, 0], HD).reshape(1, 256)
    prm["bias1"] = jnp.repeat(d_b[l][:, 0], HD).reshape(1, 256)
    wa = w_br_a[l].reshape(H, HD, D_MODEL)[jnp.array(A_HEAD_ORDER)].reshape(256, D_MODEL)
    prm["wbr"] = jnp.stack([wa, w_br_b[l], w_br_c[l], w_br_d[l]], axis=0).astype(BF16)
    prm["wout"] = w_out[l].astype(BF16)
    return prm


def kernel(x_prompt, x_sample, cache_a_k, cache_a_v, cache_b_latent, cache_b_kpe, state_c_wkv, state_c_shift, page_table, ln_g, w_in, a_q_norm, a_k_norm, b_q_a_norm, b_w_uq, b_q_norm, b_kv_a_norm, b_kpe_norm, b_w_uk, b_w_uv, c_mu, c_w0, c_w_up, c_a0, c_a_up, c_k_k, c_k_a, c_r_k, c_gn, d_v_norm, d_ws, d_b, w_br_a, w_br_b, w_br_c, w_br_d, w_out):
    depth = w_in.shape[0]
    bp, seq, _ = x_prompt.shape
    db, dec_t, _ = x_sample.shape
    n_pool = cache_a_k.shape[1]
    n_pages = page_table.shape[1]
    past = n_pages * PAGE
    assert dec_t == 1 and past % MOBA_BLOCK == 0 and past // MOBA_BLOCK >= MOBA_TOPK
    assert seq % MOBA_BLOCK == 0 and seq // MOBA_BLOCK <= LANES
    n = bp * seq
    nblk = seq // MOBA_BLOCK

    ones_bd = jnp.asarray(np.kron(np.eye(4), np.ones((HD, HD))), BF16)
    pos_p = jnp.tile(jnp.arange(seq, dtype=jnp.int32), bp)
    pos_s = jnp.full((db,), past, jnp.int32)
    tabs_a_p, tabs_a_s = _rope_tables(pos_p, 0, HD), _rope_tables(pos_s, 0, HD)
    tabs_b_p, tabs_b_s = _rope_tables(pos_p, 64, ROPE_B), _rope_tables(pos_s, 64, ROPE_B)

    ck = cache_a_k.reshape(depth * n_pool, PAGE, LANES)
    cv = cache_a_v.reshape(depth * n_pool, PAGE, LANES)
    cl = cache_b_latent.reshape(depth * n_pool, PAGE, KV_LORA)
    cp = cache_b_kpe.reshape(depth * n_pool, PAGE, ROPE_B)

    tm_prompt = min(1024, n)
    pages_per_step = min(16, n_pages)
    tm_prep = min(2048, n)
    hp = x_prompt.reshape(n, D_MODEL)
    hs = x_sample.reshape(db, D_MODEL)
    outs = [[] for _ in range(13)]
    for l in range(depth):
        prm = _layer_params(l, a_q_norm, a_k_norm, b_q_a_norm, b_w_uq, b_q_norm, b_kv_a_norm, b_kpe_norm, b_w_uk,
                            b_w_uv, c_mu, c_w0, c_w_up, c_a0, c_a_up, c_k_k, c_k_a, c_r_k, c_gn, d_v_norm, d_ws, d_b,
                            w_br_a, w_br_b, w_br_c, w_br_d, w_out)
        w_packed = _pack_w_in(w_in[l])

        pp = _proj(hp, ln_g[l], w_packed, tm_prompt, 1280)
        qa, ka, kab, vab, kmean = _moba_prep(pp, prm["gq"], prm["gk"], tabs_a_p, ones_bd, tm_prep, True)
        km_pad = jnp.zeros((bp, LANES, LANES), F32).at[:, :nblk].set(kmean.reshape(bp, nblk, LANES))
        oa = _moba_attn(qa, kab, vab, km_pad, bp, seq)
        qbb, kbb, vbb, lat, kpe = _mla_prep(pp, prm["gqa"], prm["wuq"], prm["gqb"], prm["gkv"], prm["gkpe"],
                                            prm["wuk"], prm["wuv"], tabs_b_p, tm_prep, False)
        ob = _mla_attn(qbb, kbb, vbb, bp, seq, min(256, seq))
        oc, wkv, last = _rwkv_seq(pp, prm["rwkv"], ones_bd, bp, seq, min(256, seq))
        (hp_new,) = _merge(hp, pp, oa, ob, oc, prm["vn"], prm["ws"], prm["bias"], prm["wbr"], prm["wout"],
                           min(256, n), CHUNK_D)
        outs[0].append(ka.reshape(bp, seq, 2, HD))
        outs[1].append(pp[:, P_A + 384:P_A + 512].reshape(bp, seq, 2, HD))
        outs[2].append(lat.reshape(bp, seq, KV_LORA))
        outs[3].append(kpe[:, 64:64 + ROPE_B].reshape(bp, seq, ROPE_B))
        wkv4 = wkv.reshape(bp, H, HD, H, HD)
        outs[4].append(jnp.stack([wkv4[:, h, :, h, :] for h in range(H)], axis=1))
        outs[5].append(last[:, 0, :C_SHIFT])
        hp = hp_new

        ps = _proj(hs, ln_g[l], w_packed, db, 1280)
        qa, ka, _, _ = _moba_prep(ps, prm["gq"], prm["gk"], tabs_a_s, ones_bd, db, False)
        va = ps[:, P_A + 384:P_A + 512]
        q4 = qa.reshape(db, 2, 2, HD)
        zeros = jnp.zeros((db, HD), F32)
        q8 = jnp.stack([
            jnp.concatenate([q4[:, 0, 0], zeros], axis=1), jnp.concatenate([q4[:, 1, 0], zeros], axis=1),
            jnp.concatenate([zeros, q4[:, 1, 1]], axis=1), jnp.concatenate([zeros, q4[:, 0, 1]], axis=1),
        ] + [jnp.zeros((db, LANES), F32)] * 4, axis=1)
        idx = _moba_pick(page_table, q8, ck, l * n_pool, pages_per_step)
        idx12 = idx[:, :H, :MOBA_TOPK].reshape(db, H * MOBA_TOPK)
        o8 = _moba_decode(idx12, page_table, q8, ka, va, ck, cv, l * n_pool)
        oa_s = jnp.concatenate([o8[:, 0, :HD], o8[:, 3, HD:], o8[:, 1, :HD], o8[:, 2, HD:]], axis=1)

        qlat, qpe, snew, lat_s, kpe_s = _mla_prep(ps, prm["gqa"], prm["wuq"], prm["gqb"], prm["gkv"], prm["gkpe"],
                                                  prm["wuk_t"], prm["wuv"], tabs_b_s, db, True)
        pad8 = lambda x: jnp.concatenate([x, jnp.zeros((db, 8 - H) + x.shape[2:], x.dtype)], axis=1)
        qlat8 = pad8(qlat.reshape(db, H, LANES))
        qpe8 = pad8(qpe.reshape(db, H, LANES)[:, :, 64:64 + ROPE_B])
        snew8 = pad8(snew[:, :H].reshape(db, H, 1))
        olat8 = _mla_decode(page_table, qlat8, qpe8, snew8, lat_s, cl, cp, l * n_pool, pages_per_step)
        ob_s = _mla_up(olat8[:, :H].reshape(db, H * LANES), prm["wuv_heads"])

        prev = _lane_pad(state_c_shift[l], 0, 1024)
        oc_s, wkv_s = _rwkv_step(ps, prev, state_c_wkv[l], prm["rwkv"], ones_bd, 8)
        hs_new, v_rows = _merge(hs, ps, oa_s, ob_s, oc_s, prm["vn"], prm["ws1"], prm["bias1"], prm["wbr"],
                                prm["wout"], db, 1)
        outs[6].append(ka.reshape(db, 1, 2, HD))
        outs[7].append(va.reshape(db, 1, 2, HD))
        outs[8].append(lat_s.reshape(db, 1, KV_LORA))
        outs[9].append(kpe_s[:, 64:64 + ROPE_B].reshape(db, 1, ROPE_B))
        outs[10].append(wkv_s)
        outs[11].append(ps[:, P_C:P_C + C_SHIFT])
        outs[12].append(v_rows.reshape(db, 1, 256))
        hs = hs_new

    st = lambda xs: jnp.stack(xs, axis=0)
    return (hp.reshape(bp, seq, D_MODEL), hs.reshape(db, 1, D_MODEL)) + tuple(st(o) for o in outs)
```

```python
import functools

import numpy as np
import jax
import jax.numpy as jnp
from jax import lax
from jax.experimental import pallas as pl
from jax.experimental.pallas import tpu as pltpu

F32 = jnp.float32
BF16 = jnp.bfloat16
HIGHEST = lax.Precision.HIGHEST

D_MODEL = 1024
PAGE = 128
ROPE_THETA = 10000.0
NORM_EPS = 1e-6
GN_EPS = 64e-5
HD = 64
H = 4
W_BR = 256
MOBA_BLOCK = 256
MOBA_TOPK = 3
Q_LORA = 256
KV_LORA = 128
NOPE_B = 64
ROPE_B = 32
C_SHIFT = 896
CHUNK_D = 128
CHUNK_C = 64
NEG = -1e30
LANES = 128

P_MG, P_GATES, P_A, P_B, P_C, P_D, P_COLS = 0, 4096, 5120, 5632, 6144, 7168, 7680
O_AQ, O_AK, O_AV, O_AG = 0, 256, 384, 512
O_BDQ, O_BDKV, O_BG = 768, 1024, 1184
O_CX, O_CG = 1440, 2336
O_DU, O_DV, O_DG, O_MG = 2592, 2848, 3104, 3360
A_HEAD_ORDER = (0, 3, 1, 2)

VMEM_LIMIT = 56 * 1024 * 1024


def _cparams(*sem):
    return pltpu.CompilerParams(dimension_semantics=sem, vmem_limit_bytes=VMEM_LIMIT)


def _dot(a, b, precision=None):
    return jnp.dot(a, b, preferred_element_type=F32, precision=precision)


def _dot_nt(a, b, precision=None):
    return lax.dot_general(a, b, (((1,), (1,)), ((), ())), preferred_element_type=F32, precision=precision)


def _dot_tn(a, b, precision=None):
    return lax.dot_general(a, b, (((0,), (0,)), ((), ())), preferred_element_type=F32, precision=precision)


def _seg_sum(x, ones):
    hi = x.astype(BF16)
    lo = (x - hi.astype(F32)).astype(BF16)
    return _dot(hi, ones) + _dot(lo, ones)


def _rope(x, cos, sin_a, sin_b, half):
    return x * cos + pltpu.roll(x, LANES - half, 1) * sin_a + pltpu.roll(x, half, 1) * sin_b


def _sigmoid(x):
    return 1.0 / (1.0 + jnp.exp(-x))


def _silu(x):
    return x * _sigmoid(x)


def _gelu(x):
    return 0.5 * x * (1.0 + jnp.tanh(np.sqrt(2.0 / np.pi).astype(np.float32) * (x + 0.044715 * (x * x * x))))


def _proj_kernel(x_ref, g_ref, w_ref, o_ref, h_ref):
    @pl.when(pl.program_id(1) == 0)
    def _():
        x = x_ref[...]
        y = x * lax.rsqrt(jnp.mean(x * x, axis=-1, keepdims=True) + NORM_EPS)
        h_ref[...] = (y * g_ref[...]).astype(BF16)

    o_ref[...] = _dot(h_ref[...], w_ref[...])


def _proj(x, g, w, tm, tn):
    n, d = x.shape
    c = w.shape[1]
    return pl.pallas_call(
        _proj_kernel,
        grid=(n // tm, c // tn),
        in_specs=[
            pl.BlockSpec((tm, d), lambda i, j: (i, 0)),
            pl.BlockSpec((1, d), lambda i, j: (0, 0)),
            pl.BlockSpec((d, tn), lambda i, j: (0, j)),
        ],
        out_specs=pl.BlockSpec((tm, tn), lambda i, j: (i, j)),
        out_shape=jax.ShapeDtypeStruct((n, c), F32),
        scratch_shapes=[pltpu.VMEM((tm, d), BF16)],
        compiler_params=_cparams("parallel", "arbitrary"),
        name="proj",
    )(x, g.reshape(1, d), w)


def _moba_prep_kernel(a_ref, gq_ref, gk_ref, cos_ref, sa_ref, sb_ref, ones_ref, q_ref, k_ref, kb_ref, vb_ref,
                      *km_ref, blocks, tiles_per_seq):
    a = a_ref[...]
    ones = ones_ref[...]
    cos, sa, sb = cos_ref[...], sa_ref[...], sb_ref[...]
    q, k, v = a[:, :256], a[:, 256:384], a[:, 384:512]
    qn = q * lax.rsqrt(_seg_sum(q * q, ones) * (1.0 / HD) + NORM_EPS) * gq_ref[...]
    kn = k * lax.rsqrt(_seg_sum(k * k, ones[:LANES, :LANES]) * (1.0 / HD) + NORM_EPS) * gk_ref[...]
    q_ref[:, :LANES] = _rope(qn[:, :LANES], cos, sa, sb, HD // 2)
    q_ref[:, LANES:] = _rope(qn[:, LANES:], cos, sa, sb, HD // 2)
    kr = _rope(kn, cos, sa, sb, HD // 2)
    k_ref[...] = kr
    kb_ref[:, :LANES] = kr.astype(BF16)
    vb_ref[...] = v.astype(BF16)
    if blocks:
        km_ref[0][...] = jnp.mean(kr.reshape(blocks, MOBA_BLOCK, LANES), axis=1)
        tm = a.shape[0]
        first_blk = (pl.program_id(0) % tiles_per_seq) * blocks
        blk = first_blk + (lax.broadcasted_iota(jnp.int32, (tm, LANES), 0) >> 8)
        lane = lax.broadcasted_iota(jnp.int32, (tm, LANES), 1)
        kb_ref[:, LANES:] = jnp.where(lane == blk, 1.0, 0.0).astype(BF16)
    else:
        kb_ref[:, LANES:] = jnp.zeros((a.shape[0], LANES), BF16)


def _moba_prep(p, gq, gk, tabs, ones, tm, seq):
    n = p.shape[0]
    row = lambda i: (i, 0)
    fix = lambda i: (0, 0)
    with_kmean = seq > 0
    assert MOBA_BLOCK == 256 and (not with_kmean or seq % tm == 0)
    blocks = tm // MOBA_BLOCK if with_kmean else 0
    out_shape = [
        jax.ShapeDtypeStruct((n, 256), F32),
        jax.ShapeDtypeStruct((n, LANES), F32),
        jax.ShapeDtypeStruct((n, 256), BF16),
        jax.ShapeDtypeStruct((n, LANES), BF16),
    ]
    out_specs = [pl.BlockSpec((tm, 256), row), pl.BlockSpec((tm, LANES), row), pl.BlockSpec((tm, 256), row),
                 pl.BlockSpec((tm, LANES), row)]
    if with_kmean:
        out_shape.append(jax.ShapeDtypeStruct((n // MOBA_BLOCK, LANES), F32))
        out_specs.append(pl.BlockSpec((blocks, LANES), row))
    return pl.pallas_call(
        functools.partial(_moba_prep_kernel, blocks=blocks, tiles_per_seq=max(seq // tm, 1)),
        grid=(n // tm,),
        in_specs=[
            pl.BlockSpec((tm, 512), lambda i: (i, P_A // 512)),
            pl.BlockSpec((1, 256), fix),
            pl.BlockSpec((1, LANES), fix),
            pl.BlockSpec((tm, LANES), row),
            pl.BlockSpec((tm, LANES), row),
            pl.BlockSpec((tm, LANES), row),
            pl.BlockSpec((256, 256), fix),
        ],
        out_specs=out_specs,
        out_shape=out_shape,
        compiler_params=_cparams("parallel"),
        name="moba_prep",
    )(p, gq, gk, *tabs, ones)


def _softmax_first(s, v):
    m = jnp.max(s, axis=-1, keepdims=True)
    p = jnp.exp(s - m)
    return m, jnp.sum(p, axis=-1, keepdims=True), _dot(p.astype(BF16), v)


def _softmax_next(s, v, m, l, acc):
    mn = jnp.maximum(m, jnp.max(s, axis=-1, keepdims=True))
    alpha = jnp.exp(m - mn)
    p = jnp.exp(s - mn)
    return mn, alpha * l + jnp.sum(p, axis=-1, keepdims=True), alpha * acc + _dot(p.astype(BF16), v)


def _moba_attn_kernel(q_ref, k_ref, v_ref, km_ref, o_ref):
    tq = q_ref.shape[0]
    i = pl.program_id(1)
    q = q_ref[...]
    km = km_ref[0]
    lane = lax.broadcasted_iota(jnp.int32, (tq, LANES), 1)
    lane_f = lane.astype(F32)
    lower = lane < HD

    stacked = []
    for plane in range(2):
        qp = q[:, plane * LANES:(plane + 1) * LANES]
        for half in range(2):
            qm = jnp.where(lower if half == 0 else jnp.logical_not(lower), qp, 0.0)
            gs = jnp.where(lane < i, _dot_nt(qm, km, HIGHEST), NEG)
            bias = jnp.where(lane == i, 0.0, NEG)
            for _ in range(MOBA_TOPK):
                m = jnp.max(gs, axis=1, keepdims=True)
                idx = jnp.min(jnp.where(gs == m, lane_f, float(LANES)), axis=1, keepdims=True)
                pick = jnp.logical_and(lane_f == idx, m > 0.5 * NEG)
                bias = jnp.where(pick, 0.0, bias)
                gs = jnp.where(pick, NEG, gs)
            stacked.append(jnp.concatenate([(qm * (HD ** -0.5)).astype(BF16), bias.astype(BF16)], axis=1))
    qa = jnp.concatenate(stacked, axis=0)

    tk = 2 * tq
    row = lax.broadcasted_iota(jnp.int32, (H * tq, tk), 0) & (tq - 1)
    col = lax.broadcasted_iota(jnp.int32, (H * tq, tk), 1)
    r0 = pl.multiple_of((i // 2) * tk, tk)
    s = jnp.where(col <= row + (i % 2) * tq, _dot_nt(qa, k_ref[pl.ds(r0, tk), :]), NEG)
    init = _softmax_first(s, v_ref[pl.ds(r0, tk), :])

    def body(j, carry):
        c0 = pl.multiple_of(j * tk, tk)
        return _softmax_next(_dot_nt(qa, k_ref[pl.ds(c0, tk), :]), v_ref[pl.ds(c0, tk), :], *carry)

    _, l, acc = lax.fori_loop(0, i // 2, body, init)
    o = acc / l
    o_ref[:, :LANES] = jnp.where(lower, o[0:tq], o[tq:2 * tq])
    o_ref[:, LANES:] = jnp.where(lower, o[2 * tq:3 * tq], o[3 * tq:4 * tq])


def _moba_attn(q, kb, vb, km, batch, seq):
    nq = seq // MOBA_BLOCK
    return pl.pallas_call(
        _moba_attn_kernel,
        grid=(batch, nq),
        in_specs=[
            pl.BlockSpec((MOBA_BLOCK, 256), lambda b, i: (b * nq + i, 0)),
            pl.BlockSpec((seq, 256), lambda b, i: (b, 0)),
            pl.BlockSpec((seq, LANES), lambda b, i: (b, 0)),
            pl.BlockSpec((1, LANES, LANES), lambda b, i: (b, 0, 0)),
        ],
        out_specs=pl.BlockSpec((MOBA_BLOCK, 256), lambda b, i: (b * nq + i, 0)),
        out_shape=jax.ShapeDtypeStruct((batch * seq, 256), F32),
        compiler_params=_cparams("parallel", "arbitrary"),
        name="moba_attn",
    )(q, kb, vb, km)


def _mla_prep_kernel(b_ref, gqa_ref, wuq_ref, gq_ref, gkv_ref, gkpe_ref, wuk_ref, wuv_ref, cos_ref, sa_ref, sb_ref,
                     *out_refs, decode):
    x = b_ref[...]
    cos, sa, sb = cos_ref[...], sa_ref[...], sb_ref[...]
    dq, lat_raw, kpe_raw = x[:, :256], x[:, 256:384], x[:, 384:512]
    cq = dq * lax.rsqrt(jnp.mean(dq * dq, axis=-1, keepdims=True) + NORM_EPS) * gqa_ref[...]
    q = _dot(cq.astype(BF16), wuq_ref[...])
    lat = lat_raw * lax.rsqrt(jnp.mean(lat_raw * lat_raw, axis=-1, keepdims=True) + NORM_EPS) * gkv_ref[...]
    kpe = kpe_raw * lax.rsqrt(jnp.sum(kpe_raw * kpe_raw, axis=-1, keepdims=True) * (1.0 / ROPE_B) + NORM_EPS)
    kpe = _rope(kpe * gkpe_ref[...], cos, sa, sb, ROPE_B // 2)
    scale = (NOPE_B + ROPE_B) ** -0.5
    qs = []
    for h in range(H):
        qh = q[:, h * LANES:(h + 1) * LANES]
        qh = qh * lax.rsqrt(jnp.sum(qh * qh, axis=-1, keepdims=True) * (1.0 / (NOPE_B + ROPE_B)) + NORM_EPS)
        qs.append(_rope(qh * gq_ref[...], cos, sa, sb, ROPE_B // 2) * scale)
    latb = lat.astype(BF16)
    if decode:
        qlat_ref, qpe_ref, snew_ref, lat_ref, kpe_ref = out_refs
        lane = lax.broadcasted_iota(jnp.int32, kpe.shape, 1)
        snew = jnp.zeros(kpe.shape, F32)
        for h in range(H):
            ql = _dot(qs[h].astype(BF16), wuk_ref[h])
            qlat_ref[:, h * LANES:(h + 1) * LANES] = ql
            qpe_ref[:, h * LANES:(h + 1) * LANES] = qs[h]
            s = jnp.sum(ql.astype(BF16).astype(F32) * latb.astype(F32), axis=-1, keepdims=True)
            s = s + jnp.sum(qs[h].astype(BF16).astype(F32) * kpe.astype(BF16).astype(F32), axis=-1, keepdims=True)
            snew = jnp.where(lane == h, s, snew)
        snew_ref[...] = snew
    else:
        qb_ref, kb_ref, vb_ref, lat_ref, kpe_ref = out_refs
        kn = _dot(latb, wuk_ref[...])
        for h in range(H):
            qb_ref[:, h * LANES:(h + 1) * LANES] = qs[h].astype(BF16)
            kb_ref[:, h * LANES:(h + 1) * LANES] = (kn[:, h * LANES:(h + 1) * LANES] + kpe).astype(BF16)
        vb_ref[...] = _dot(latb, wuv_ref[...]).astype(BF16)
    lat_ref[...] = lat
    kpe_ref[...] = kpe


def _mla_prep(p, gqa, wuq, gq, gkv, gkpe, wuk, wuv, tabs, tm, decode):
    n = p.shape[0]
    row = lambda i: (i, 0)
    fix = lambda i: (0, 0)
    wide = lambda dt: (jax.ShapeDtypeStruct((n, 512), dt), pl.BlockSpec((tm, 512), row))
    plane = lambda dt: (jax.ShapeDtypeStruct((n, LANES), dt), pl.BlockSpec((tm, LANES), row))
    if decode:
        outs = [wide(F32), wide(F32), plane(F32), plane(F32), plane(F32)]
        wuk_spec = pl.BlockSpec((H, LANES, LANES), lambda i: (0, 0, 0))
    else:
        outs = [wide(BF16), wide(BF16), (jax.ShapeDtypeStruct((n, 256), BF16), pl.BlockSpec((tm, 256), row)),
                plane(F32), plane(F32)]
        wuk_spec = pl.BlockSpec((LANES, 512), fix)
    return pl.pallas_call(
        functools.partial(_mla_prep_kernel, decode=decode),
        grid=(n // tm,),
        in_specs=[
            pl.BlockSpec((tm, 512), lambda i: (i, P_B // 512)),
            pl.BlockSpec((1, 256), fix),
            pl.BlockSpec((256, 512), fix),
            pl.BlockSpec((1, LANES), fix),
            pl.BlockSpec((1, LANES), fix),
            pl.BlockSpec((1, LANES), fix),
            wuk_spec,
            pl.BlockSpec((LANES, 256), fix),
            pl.BlockSpec((tm, LANES), row),
            pl.BlockSpec((tm, LANES), row),
            pl.BlockSpec((tm, LANES), row),
        ],
        out_specs=[s for _, s in outs],
        out_shape=[s for s, _ in outs],
        compiler_params=_cparams("parallel"),
        name="mla_prep",
    )(p, gqa, wuq, gq, gkv, gkpe, wuk, wuv, *tabs)


def _mla_attn_kernel(q_ref, k_ref, v_ref, o_ref):
    tq = q_ref.shape[0]
    i = pl.program_id(1)
    lane = lax.broadcasted_iota(jnp.int32, (tq, LANES), 1)
    lower = lane < HD
    row = lax.broadcasted_iota(jnp.int32, (tq, tq), 0)
    col = lax.broadcasted_iota(jnp.int32, (tq, tq), 1)
    r0 = pl.multiple_of(i * tq, tq)
    qs = [q_ref[:, h * LANES:(h + 1) * LANES] for h in range(H)]

    def tile(c0, h):
        return (k_ref[pl.ds(c0, tq), h * LANES:(h + 1) * LANES],
                v_ref[pl.ds(c0, tq), (h // 2) * LANES:(h // 2 + 1) * LANES])

    init = []
    for h in range(H):
        kh, vh = tile(r0, h)
        init.extend(_softmax_first(jnp.where(col <= row, _dot_nt(qs[h], kh), NEG), vh))

    def body(j, carry):
        c0 = pl.multiple_of(j * tq, tq)
        out = []
        for h in range(H):
            kh, vh = tile(c0, h)
            out.extend(_softmax_next(_dot_nt(qs[h], kh), vh, *carry[3 * h:3 * h + 3]))
        return tuple(out)

    fin = lax.fori_loop(0, i, body, tuple(init))
    for plane in range(2):
        o0 = fin[6 * plane + 2] / fin[6 * plane + 1]
        o1 = fin[6 * plane + 5] / fin[6 * plane + 4]
        o_ref[:, plane * LANES:(plane + 1) * LANES] = jnp.where(lower, o0, o1)


def _mla_attn(qb, kb, vb, batch, seq, tq):
    nq = seq // tq
    return pl.pallas_call(
        _mla_attn_kernel,
        grid=(batch, nq),
        in_specs=[
            pl.BlockSpec((tq, 512), lambda b, i: (b * nq + i, 0)),
            pl.BlockSpec((seq, 512), lambda b, i: (b, 0)),
            pl.BlockSpec((seq, 256), lambda b, i: (b, 0)),
        ],
        out_specs=pl.BlockSpec((tq, 256), lambda b, i: (b * nq + i, 0)),
        out_shape=jax.ShapeDtypeStruct((batch * seq, 256), F32),
        compiler_params=_cparams("parallel", "arbitrary"),
        name="mla_attn",
    )(qb, kb, vb)


def _rwkv_features(xc, prev, mu, w0, wup, a0, aup, kk_w, ka_w, ones):
    xs = xc + (prev - xc) * mu
    r, k, v, lora = xs[:, :256], xs[:, 256:512], xs[:, 512:768], xs[:, 768:896]
    w_pre = w0 + _dot(jnp.tanh(lora).astype(BF16), wup)
    z = -w_pre
    w_log = -(jnp.maximum(z, 0.0) + jnp.log(1.0 + jnp.exp(-jnp.abs(z)))) - 0.5
    lw = -jnp.exp(w_log)
    a = _sigmoid(a0 + _dot(lora.astype(BF16), aup))
    kk = k * kk_w
    kk = kk * lax.rsqrt(_seg_sum(kk * kk, ones) + 1e-12)
    k2 = k * (1.0 + (a - 1.0) * ka_w)
    return r, k2, v, kk, kk * a, lw


def _rwkv_out(y, r, k2, v, rk_w, gn_w, ones):
    mean = _seg_sum(y, ones) * (1.0 / HD)
    d = y - mean
    var = _seg_sum(d * d, ones) * (1.0 / HD)
    return d * lax.rsqrt(var + GN_EPS) * gn_w + _seg_sum(r * k2 * rk_w, ones) * v


def _block_stack(x, same_blk):
    return jnp.where(same_blk, jnp.concatenate([x, x, x, x], axis=0), 0.0)


def _row_blocks_sum(x):
    c = CHUNK_C
    return x[0:c] + x[c:2 * c] + x[2 * c:3 * c] + x[3 * c:4 * c]


def _rwkv_seq_kernel(x_ref, mu_ref, w0_ref, wup_ref, a0_ref, aup_ref, kk_ref, ka_ref, rk_ref, gn_ref, ones_ref,
                     o_ref, s_out_ref, last_ref, s_ref, prev_ref):
    tt = x_ref.shape[0]
    c = CHUNK_C

    @pl.when(pl.program_id(1) == 0)
    def _():
        s_ref[...] = jnp.zeros(s_ref.shape, F32)
        prev_ref[...] = jnp.zeros(prev_ref.shape, F32)

    ones = ones_ref[...]
    xc = x_ref[...]
    rid = lax.broadcasted_iota(jnp.int32, xc.shape, 0)
    prev = jnp.where(rid == 0, prev_ref[...], pltpu.roll(xc, 1, 0))
    prev_ref[...] = xc[tt - 1:tt, :]
    last_ref[0] = xc[tt - 1:tt, :]
    r, k2, v, kk, bv, lw = _rwkv_features(xc, prev, mu_ref[...], w0_ref[...], wup_ref[...], a0_ref[...], aup_ref[...],
                                          kk_ref[...], ka_ref[...], ones)

    rr = lax.broadcasted_iota(jnp.int32, (256, 256), 0)
    cc = lax.broadcasted_iota(jnp.int32, (256, 256), 1)
    same_blk = (rr >> 6) == (cc >> 6)
    strict = (rr & 63) > (cc & 63)
    incl = (rr & 63) >= (cc & 63)
    eye = jnp.where(rr == cc, 1.0, 0.0)
    tri = jnp.where(incl[:c, :c], 1.0, 0.0)

    chunks = []
    for ci in range(tt // c):
        sl = slice(ci * c, (ci + 1) * c)
        lw_c, r_c, k_c, v_c, kk_c, bv_c = lw[sl], r[sl], k2[sl], v[sl], kk[sl], bv[sl]
        cum = _dot(tri, lw_c, HIGHEST)
        last = cum[c - 1:c, :]
        e_neg = jnp.exp(-cum)
        e_end = jnp.exp(last - cum)
        chunks.append(dict(
            a=_block_stack(-kk_c * jnp.exp(cum - lw_c), same_blk).astype(BF16),
            r=_block_stack(r_c * jnp.exp(cum), same_blk).astype(BF16),
            b=_block_stack(bv_c * e_neg, same_blk).astype(BF16),
            k=_block_stack(k_c * e_neg, same_blk).astype(BF16),
            v=_block_stack(v_c, same_blk).astype(BF16),
            v_c=v_c, bk=jnp.concatenate([bv_c * e_end, k_c * e_end], axis=0).astype(BF16), decay=jnp.exp(last)))
    for ch in chunks:
        ch["n_ab"] = jnp.where(strict, _dot_nt(ch["a"], ch["b"]), 0.0)
        ch["n_ak"] = jnp.where(strict, _dot_nt(ch["a"], ch["k"]), 0.0).astype(BF16)
        ch["n_rb"] = jnp.where(incl, _dot_nt(ch["r"], ch["b"]), 0.0).astype(BF16)
        ch["n_rk"] = jnp.where(incl, _dot_nt(ch["r"], ch["k"]), 0.0).astype(BF16)
        ch["t_inv"] = eye + ch["n_ab"]
        ch["pw"] = ch["n_ab"].astype(BF16)
    for _ in range(5):
        for ch in chunks:
            ch["pw"] = _dot(ch["pw"], ch["pw"]).astype(BF16)
        for ch in chunks:
            ch["t_inv"] = ch["t_inv"] + _dot(ch["t_inv"].astype(BF16), ch["pw"])
    for ch in chunks:
        ch["t_inv"] = ch["t_inv"].astype(BF16)
        ch["akv"] = _dot(ch["n_ak"], ch["v"])
        ch["rkv"] = _dot(ch["n_rk"], ch["v"])

    s = s_ref[...]
    ys = []
    for ch in chunks:
        sb = s.astype(BF16)
        u_bd = _dot(ch["t_inv"], (_dot_nt(ch["a"], sb) + ch["akv"]).astype(BF16))
        y_bd = _dot_nt(ch["r"], sb) + _dot(ch["n_rb"], u_bd.astype(BF16)) + ch["rkv"]
        ys.append(_row_blocks_sum(y_bd))
        uv = jnp.concatenate([_row_blocks_sum(u_bd), ch["v_c"]], axis=0).astype(BF16)
        s = s * ch["decay"] + jnp.where(same_blk, _dot_tn(uv, ch["bk"]), 0.0)
    s_ref[...] = s
    s_out_ref[0] = s
    y = jnp.concatenate(ys, axis=0)
    o_ref[...] = _rwkv_out(y, r, k2, v, rk_ref[...], gn_ref[...], ones)


def _rwkv_seq(p, prm, ones, batch, seq, tt):
    nt = seq // tt
    fix = lambda b, t: (0, 0)
    vec256 = pl.BlockSpec((1, 256), fix)
    return pl.pallas_call(
        _rwkv_seq_kernel,
        grid=(batch, nt),
        in_specs=[
            pl.BlockSpec((tt, 1024), lambda b, t: (b * nt + t, P_C // 1024)),
            pl.BlockSpec((1, 1024), fix),
            vec256, pl.BlockSpec((LANES, 256), fix), vec256, pl.BlockSpec((LANES, 256), fix),
            vec256, vec256, vec256, vec256,
            pl.BlockSpec((256, 256), fix),
        ],
        out_specs=[
            pl.BlockSpec((tt, 256), lambda b, t: (b * nt + t, 0)),
            pl.BlockSpec((1, 256, 256), lambda b, t: (b, 0, 0)),
            pl.BlockSpec((1, 1, 1024), lambda b, t: (b, 0, 0)),
        ],
        out_shape=[
            jax.ShapeDtypeStruct((batch * seq, 256), F32),
            jax.ShapeDtypeStruct((batch, 256, 256), F32),
            jax.ShapeDtypeStruct((batch, 1, 1024), F32),
        ],
        scratch_shapes=[pltpu.VMEM((256, 256), F32), pltpu.VMEM((1, 1024), F32)],
        compiler_params=_cparams("parallel", "arbitrary"),
        name="rwkv_seq",
    )(p, prm["mu"], prm["w0"], prm["wup"], prm["a0"], prm["aup"], prm["kk"], prm["ka"], prm["rk"], prm["gn"], ones)


def _rwkv_step_kernel(x_ref, prev_ref, s_ref, mu_ref, w0_ref, wup_ref, a0_ref, aup_ref, kk_ref, ka_ref, rk_ref,
                      gn_ref, ones_ref, o_ref, s_out_ref, yt_ref):
    nb = x_ref.shape[0]
    ones = ones_ref[...]
    r, k2, v, kk, bv, lw = _rwkv_features(x_ref[...], prev_ref[...], mu_ref[...], w0_ref[...], wup_ref[...],
                                          a0_ref[...], aup_ref[...], kk_ref[...], ka_ref[...], ones)
    w = jnp.exp(lw)
    vt = v.T
    for b in range(nb):
        for h in range(H):
            hs = slice(h * HD, (h + 1) * HD)
            st = s_ref[b, h]
            sa = -jnp.sum(st * kk[b:b + 1, hs], axis=-1, keepdims=True)
            st = st * w[b:b + 1, hs] + sa * bv[b:b + 1, hs] + vt[hs, b:b + 1] * k2[b:b + 1, hs]
            s_out_ref[b, h] = st
            yt_ref[hs, b:b + 1] = jnp.sum(st * r[b:b + 1, hs], axis=-1, keepdims=True)
    y = yt_ref[...].T
    o_ref[...] = _rwkv_out(y, r, k2, v, rk_ref[...], gn_ref[...], ones)


def _rwkv_step(p, prev, state, prm, ones, nb):
    n = p.shape[0]
    fix = lambda i: (0, 0)
    vec256 = pl.BlockSpec((1, 256), fix)
    return pl.pallas_call(
        _rwkv_step_kernel,
        grid=(n // nb,),
        in_specs=[
            pl.BlockSpec((nb, 1024), lambda i: (i, P_C // 1024)),
            pl.BlockSpec((nb, 1024), lambda i: (i, 0)),
            pl.BlockSpec((nb, H, HD, HD), lambda i: (i, 0, 0, 0)),
            pl.BlockSpec((1, 1024), fix),
            vec256, pl.BlockSpec((LANES, 256), fix), vec256, pl.BlockSpec((LANES, 256), fix),
            vec256, vec256, vec256, vec256,
            pl.BlockSpec((256, 256), fix),
        ],
        out_specs=[
            pl.BlockSpec((nb, 256), lambda i: (i, 0)),
            pl.BlockSpec((nb, H, HD, HD), lambda i: (i, 0, 0, 0)),
        ],
        out_shape=[
            jax.ShapeDtypeStruct((n, 256), F32),
            jax.ShapeDtypeStruct((n, H, HD, HD), F32),
        ],
        scratch_shapes=[pltpu.VMEM((256, nb), F32)],
        compiler_params=_cparams("parallel"),
        name="rwkv_step",
    )(p, prev, state, prm["mu"], prm["w0"], prm["wup"], prm["a0"], prm["aup"], prm["kk"], prm["ka"], prm["rk"],
      prm["gn"], ones)


def _merge_kernel(x_ref, mg_ref, g_ref, oa_ref, ob_ref, oc_ref, d_ref, vn_ref, ws_ref, bias_ref, wbr_ref, wout_ref,
                  o_ref, *v_ref, chunk):
    tm = x_ref.shape[0]
    d = d_ref[...]
    u = _gelu(d[:, :256])
    vg = _gelu(d[:, 256:512])
    v = vg * lax.rsqrt(jnp.mean(vg * vg, axis=-1, keepdims=True) + NORM_EPS) * vn_ref[...]
    if chunk == 1:
        sp = v * ws_ref[...] + bias_ref[...]
        v_ref[0][...] = v
    else:
        rr = lax.broadcasted_iota(jnp.int32, (chunk, chunk), 0)
        cc = lax.broadcasted_iota(jnp.int32, (chunk, chunk), 1)
        grp = lax.broadcasted_iota(jnp.int32, (chunk, 256), 1) >> 6
        vb = v.astype(BF16)
        ws = [jnp.where(cc <= rr, ws_ref[g], 0.0).astype(BF16) for g in range(H)]
        parts = []
        for ci in range(tm // chunk):
            vc = vb[ci * chunk:(ci + 1) * chunk]
            acc = bias_ref[...]
            for g in range(H):
                acc = acc + jnp.where(grp == g, _dot(ws[g], vc), 0.0)
            parts.append(acc)
        sp = jnp.concatenate(parts, axis=0)
    od = u * sp
    gates = g_ref[...]
    y = jnp.zeros((tm, D_MODEL), F32)
    for bi, o in enumerate((oa_ref[...], ob_ref[...], oc_ref[...], od)):
        z = (o * _silu(gates[:, bi * 256:(bi + 1) * 256])).astype(BF16)
        y = y + _sigmoid(mg_ref[:, bi * D_MODEL:(bi + 1) * D_MODEL]) * _dot(z, wbr_ref[bi])
    o_ref[...] = x_ref[...] + _dot(y.astype(BF16), wout_ref[...])


def _merge(x, p, oa, ob, oc, vn, ws, bias, wbr, wout, tm, chunk):
    n = x.shape[0]
    row = lambda i: (i, 0)
    fix = lambda i: (0, 0)
    out_shape = [jax.ShapeDtypeStruct((n, D_MODEL), F32)]
    out_specs = [pl.BlockSpec((tm, D_MODEL), row)]
    if chunk == 1:
        out_shape.append(jax.ShapeDtypeStruct((n, 256), F32))
        out_specs.append(pl.BlockSpec((tm, 256), row))
        ws_spec = pl.BlockSpec((1, 256), fix)
        bias_spec = pl.BlockSpec((1, 256), fix)
    else:
        ws_spec = pl.BlockSpec((H, chunk, chunk), lambda i: (0, 0, 0))
        bias_spec = pl.BlockSpec((chunk, 256), fix)
    return pl.pallas_call(
        functools.partial(_merge_kernel, chunk=chunk),
        grid=(n // tm,),
        in_specs=[
            pl.BlockSpec((tm, D_MODEL), row),
            pl.BlockSpec((tm, 4096), lambda i: (i, P_MG // 4096)),
            pl.BlockSpec((tm, 1024), lambda i: (i, P_GATES // 1024)),
            pl.BlockSpec((tm, 256), row),
            pl.BlockSpec((tm, 256), row),
            pl.BlockSpec((tm, 256), row),
            pl.BlockSpec((tm, 512), lambda i: (i, P_D // 512)),
            pl.BlockSpec((1, 256), fix),
            ws_spec,
            bias_spec,
            pl.BlockSpec((H, 256, D_MODEL), lambda i: (0, 0, 0)),
            pl.BlockSpec((D_MODEL, D_MODEL), fix),
        ],
        out_specs=out_specs,
        out_shape=out_shape,
        compiler_params=_cparams("parallel"),
        name="merge",
    )(x, p, p, oa, ob, oc, p, vn, ws, bias, wbr, wout)


def _page_pipeline(b, n_rows, copies_of, n_copies):
    slot = b % 2

    def start_row(row, sl):
        def body(i, c):
            for cp in copies_of(row, sl, i):
                cp.start()
            return c
        lax.fori_loop(0, n_copies, body, 0)

    @pl.when(b == 0)
    def _():
        start_row(0, 0)

    @pl.when(b + 1 < n_rows)
    def _():
        start_row(b + 1, 1 - slot)

    def wait_body(i, c):
        for cp in copies_of(b, slot, i):
            cp.wait()
        return c
    lax.fori_loop(0, n_copies, wait_body, 0)
    return slot


def _moba_pick_kernel(pt_ref, q_ref, k_hbm, idx_ref, k_buf, sem, *, base, n_pages):
    b = pl.program_id(0)

    def copies_of(row, sl, i):
        off = pl.multiple_of(i * PAGE, PAGE)
        return (pltpu.make_async_copy(k_hbm.at[base + pt_ref[row, i]], k_buf.at[sl, :, pl.ds(off, PAGE)],
                                      sem.at[sl]),)

    slot = _page_pipeline(b, pl.num_programs(0), copies_of, n_pages)
    nblk = n_pages * PAGE // MOBA_BLOCK
    q = q_ref[0]
    hi = q.astype(BF16).astype(F32)
    q2 = jnp.concatenate([hi, q - hi], axis=0).astype(BF16)
    s2 = _dot(q2, k_buf[slot].astype(BF16))
    s = s2[:8] + s2[8:]
    lane = lax.broadcasted_iota(jnp.int32, (8, LANES), 1)
    lane_f = lane.astype(F32)
    gs = jnp.full((8, LANES), NEG, F32)
    for blk in range(nblk):
        tot = jnp.sum(s[:, blk * MOBA_BLOCK:(blk + 1) * MOBA_BLOCK], axis=-1, keepdims=True)
        gs = jnp.where(lane == blk, tot * (1.0 / MOBA_BLOCK), gs)
    out = jnp.zeros((8, LANES), jnp.int32)
    for t in range(MOBA_TOPK):
        m = jnp.max(gs, axis=1, keepdims=True)
        idx = jnp.min(jnp.where(gs == m, lane_f, float(LANES)), axis=1, keepdims=True)
        out = jnp.where(lane == t, idx.astype(jnp.int32), out)
        gs = jnp.where(lane_f == idx, NEG, gs)
    idx_ref[0] = out


def _moba_pick(page_table, q8, cache_kt, layer_base):
    nb, n_pages = page_table.shape
    assert n_pages * PAGE // MOBA_BLOCK <= LANES
    grid_spec = pltpu.PrefetchScalarGridSpec(
        num_scalar_prefetch=1,
        grid=(nb,),
        in_specs=[pl.BlockSpec((1, 8, LANES), lambda b, pt: (b, 0, 0)), pl.BlockSpec(memory_space=pl.ANY)],
        out_specs=pl.BlockSpec((1, 8, LANES), lambda b, pt: (b, 0, 0)),
        scratch_shapes=[pltpu.VMEM((2, LANES, n_pages * PAGE), F32), pltpu.SemaphoreType.DMA((2,))],
    )
    return pl.pallas_call(
        functools.partial(_moba_pick_kernel, base=layer_base, n_pages=n_pages),
        grid_spec=grid_spec,
        out_shape=jax.ShapeDtypeStruct((nb, 8, LANES), jnp.int32),
        compiler_params=_cparams("arbitrary"),
        name="moba_pick",
    )(page_table, q8, cache_kt)


def _moba_decode_kernel(idx_ref, pt_ref, q_ref, kn_ref, vn_ref, k_hbm, v_hbm, o_ref, k_buf, v_buf, sem, *, base):
    b = pl.program_id(0)
    per_blk = MOBA_BLOCK // PAGE
    per_head = MOBA_TOPK * per_blk

    def copies_of(row, sl, i):
        out = []
        for h in range(H):
            for t in range(MOBA_TOPK):
                for pi in range(per_blk):
                    page = base + pt_ref[row, idx_ref[row, h * MOBA_TOPK + t] * per_blk + pi]
                    dst = pl.ds((t * per_blk + pi) * PAGE, PAGE)
                    out.append(pltpu.make_async_copy(k_hbm.at[page], k_buf.at[sl, h, :, dst], sem.at[0, sl]))
                    out.append(pltpu.make_async_copy(v_hbm.at[page], v_buf.at[sl, h, :, dst], sem.at[1, sl]))
        return out

    slot = _page_pipeline(b, pl.num_programs(0), copies_of, 1)
    q = q_ref[0]
    qb = (q * (HD ** -0.5)).astype(BF16)
    kn = kn_ref[0].astype(BF16).astype(F32)
    vn = vn_ref[0].astype(BF16).astype(F32)
    s_own = jnp.sum(qb.astype(F32) * kn, axis=-1, keepdims=True)
    rowid = lax.broadcasted_iota(jnp.int32, (8, LANES), 0)
    out = jnp.zeros((8, LANES), F32)
    for h in range(H):
        s = _dot(qb, k_buf[slot, h].astype(BF16))
        m = jnp.maximum(s_own, jnp.max(s, axis=-1, keepdims=True))
        p_own = jnp.exp(s_own - m)
        p = jnp.exp(s - m)
        l = p_own + jnp.sum(p, axis=-1, keepdims=True)
        acc = p_own.astype(BF16).astype(F32) * vn + _dot_nt(p.astype(BF16), v_buf[slot, h].astype(BF16))
        out = jnp.where(rowid == h, acc / l, out)
    o_ref[0] = out


def _moba_decode(idx, page_table, q8, k_new, v_new, cache_kt, cache_vt, layer_base):
    nb = page_table.shape[0]
    sel_tokens = MOBA_TOPK * MOBA_BLOCK
    vec = pl.BlockSpec((1, 1, LANES), lambda b, idx_r, pt: (b, 0, 0))
    grid_spec = pltpu.PrefetchScalarGridSpec(
        num_scalar_prefetch=2,
        grid=(nb,),
        in_specs=[pl.BlockSpec((1, 8, LANES), lambda b, idx_r, pt: (b, 0, 0)), vec, vec,
                  pl.BlockSpec(memory_space=pl.ANY), pl.BlockSpec(memory_space=pl.ANY)],
        out_specs=pl.BlockSpec((1, 8, LANES), lambda b, idx_r, pt: (b, 0, 0)),
        scratch_shapes=[pltpu.VMEM((2, H, LANES, sel_tokens), F32), pltpu.VMEM((2, H, LANES, sel_tokens), F32),
                        pltpu.SemaphoreType.DMA((2, 2))],
    )
    return pl.pallas_call(
        functools.partial(_moba_decode_kernel, base=layer_base),
        grid_spec=grid_spec,
        out_shape=jax.ShapeDtypeStruct((nb, 8, LANES), F32),
        compiler_params=_cparams("arbitrary"),
        name="moba_decode",
    )(idx, page_table, q8, k_new.reshape(nb, 1, LANES), v_new.reshape(nb, 1, LANES), cache_kt, cache_vt)


def _mla_decode_kernel(pt_ref, ql_ref, qp_ref, sn_ref, ln_ref, lat_hbm, kpe_hbm, o_ref, lat_buf, kpe_buf, sem, *,
                       base, n_pages):
    b = pl.program_id(0)

    def copies_of(row, sl, i):
        page = base + pt_ref[row, i]
        off = pl.multiple_of(i * PAGE, PAGE)
        return (pltpu.make_async_copy(lat_hbm.at[page], lat_buf.at[sl, pl.ds(off, PAGE), :], sem.at[0, sl]),
                pltpu.make_async_copy(kpe_hbm.at[page], kpe_buf.at[sl, :, pl.ds(off, PAGE)], sem.at[1, sl]))

    slot = _page_pipeline(b, pl.num_programs(0), copies_of, n_pages)
    ql = ql_ref[0].astype(BF16)
    qp = qp_ref[0].astype(BF16)
    lat = lat_buf[slot].astype(BF16)
    s = _dot_nt(ql, lat) + _dot(qp, kpe_buf[slot].astype(BF16))
    s_new = sn_ref[0]
    m = jnp.maximum(s_new, jnp.max(s, axis=-1, keepdims=True))
    p_new = jnp.exp(s_new - m)
    p = jnp.exp(s - m).astype(BF16)
    l = p_new + jnp.sum(p.astype(F32), axis=-1, keepdims=True)
    acc = p_new.astype(BF16).astype(F32) * ln_ref[0].astype(BF16).astype(F32) + _dot(p, lat)
    o_ref[0] = acc / l


def _mla_decode(page_table, qlat8, qpe8, snew8, lat_new, cache_lat, cache_kpet, layer_base):
    nb, n_pages = page_table.shape
    per_b = lambda shape: pl.BlockSpec((1,) + shape, lambda b, pt: (b, 0, 0))
    grid_spec = pltpu.PrefetchScalarGridSpec(
        num_scalar_prefetch=1,
        grid=(nb,),
        in_specs=[per_b((8, LANES)), per_b((8, ROPE_B)), per_b((8, 1)), per_b((1, LANES)),
                  pl.BlockSpec(memory_space=pl.ANY), pl.BlockSpec(memory_space=pl.ANY)],
        out_specs=per_b((8, LANES)),
        scratch_shapes=[pltpu.VMEM((2, n_pages * PAGE, KV_LORA), F32), pltpu.VMEM((2, ROPE_B, n_pages * PAGE), F32),
                        pltpu.SemaphoreType.DMA((2, 2))],
    )
    return pl.pallas_call(
        functools.partial(_mla_decode_kernel, base=layer_base, n_pages=n_pages),
        grid_spec=grid_spec,
        out_shape=jax.ShapeDtypeStruct((nb, 8, LANES), F32),
        compiler_params=_cparams("arbitrary"),
        name="mla_decode",
    )(page_table, qlat8, qpe8, snew8, lat_new.reshape(nb, 1, LANES), cache_lat, cache_kpet)


def _mla_up_kernel(o_ref, w_ref, out_ref):
    acc = jnp.zeros(out_ref.shape, F32)
    for h in range(H):
        acc = acc + _dot(o_ref[:, h * LANES:(h + 1) * LANES].astype(BF16), w_ref[h])
    out_ref[...] = acc


def _mla_up(o_lat, wuv_heads):
    n = o_lat.shape[0]
    return pl.pallas_call(
        _mla_up_kernel,
        out_shape=jax.ShapeDtypeStruct((n, 256), F32),
        name="mla_up",
    )(o_lat, wuv_heads)


def _pack_w_in(w):
    d = w.shape[0]
    z = lambda n: jnp.zeros((d, n), w.dtype)
    col = lambda o, n: w[:, o:o + n]
    aq = jnp.concatenate([col(O_AQ + h * HD, HD) for h in A_HEAD_ORDER], axis=1)
    ag = jnp.concatenate([col(O_AG + h * HD, HD) for h in A_HEAD_ORDER], axis=1)
    parts = [
        col(O_MG, 4096),
        ag, col(O_BG, 256), col(O_CG, 256), col(O_DG, 256),
        aq, col(O_AK, 128), col(O_AV, 128),
        col(O_BDQ, 256), col(O_BDKV, KV_LORA), z(64), col(O_BDKV + KV_LORA, ROPE_B), z(32),
        col(O_CX, C_SHIFT), z(128),
        col(O_DU, 256), col(O_DV, 256),
    ]
    out = jnp.concatenate(parts, axis=1).astype(BF16)
    assert out.shape[1] == P_COLS
    return out


def _rope_tables(pos, lane_lo, dim):
    half = dim // 2
    inv = ROPE_THETA ** (-jnp.arange(half, dtype=F32) / half)
    ang = pos.astype(F32)[:, None] * inv[None, :]
    cos, sin = jnp.cos(ang), jnp.sin(ang)
    lane = np.arange(LANES)
    span = LANES if dim == HD else dim
    active = (lane >= lane_lo) & (lane < lane_lo + span)
    rel = (lane - lane_lo) % dim
    first = rel < half
    gather = rel % half
    cos_t = jnp.where(active[None, :], cos[:, gather], 1.0)
    sin_t = sin[:, gather]
    sin_a = jnp.where((active & first)[None, :], -sin_t, 0.0)
    sin_b = jnp.where((active & ~first)[None, :], sin_t, 0.0)
    return cos_t, sin_a, sin_b


def _lane_pad(x, lo, width=LANES):
    out = jnp.zeros(x.shape[:-1] + (width,), x.dtype)
    return out.at[..., lo:lo + x.shape[-1]].set(x)


def _layer_params(l, a_q_norm, a_k_norm, b_q_a_norm, b_w_uq, b_q_norm, b_kv_a_norm, b_kpe_norm, b_w_uk, b_w_uv,
                  c_mu, c_w0, c_w_up, c_a0, c_a_up, c_k_k, c_k_a, c_r_k, c_gn, d_v_norm, d_ws, d_b,
                  w_br_a, w_br_b, w_br_c, w_br_d, w_out):
    prm = {}
    prm["gq"] = jnp.tile(a_q_norm[l], 4).reshape(1, 256)
    prm["gk"] = jnp.tile(a_k_norm[l], 2).reshape(1, LANES)
    prm["gqa"] = b_q_a_norm[l].reshape(1, Q_LORA)
    prm["wuq"] = _lane_pad(b_w_uq[l], 0).reshape(Q_LORA, H * LANES).astype(BF16)
    prm["gqb"] = _lane_pad(b_q_norm[l], 0).reshape(1, LANES)
    prm["gkv"] = b_kv_a_norm[l].reshape(1, KV_LORA)
    prm["gkpe"] = _lane_pad(b_kpe_norm[l], 64).reshape(1, LANES)
    prm["wuk"] = _lane_pad(b_w_uk[l], 0).reshape(KV_LORA, H * LANES).astype(BF16)
    wukt = jnp.transpose(b_w_uk[l], (1, 2, 0))
    prm["wuk_t"] = jnp.concatenate([wukt, jnp.zeros((H, LANES - NOPE_B, KV_LORA), F32)], axis=1).astype(BF16)
    prm["wuv"] = b_w_uv[l].reshape(KV_LORA, H * HD).astype(BF16)
    wuv_h = jnp.zeros((H, KV_LORA, 256), F32)
    for h in range(H):
        wuv_h = wuv_h.at[h, :, h * HD:(h + 1) * HD].set(b_w_uv[l][:, h, :])
    prm["wuv_heads"] = wuv_h.astype(BF16)
    rw = {}
    rw["mu"] = _lane_pad(c_mu[l], 0, 1024).reshape(1, 1024)
    rw["w0"] = c_w0[l].reshape(1, 256)
    rw["wup"] = jnp.concatenate([c_w_up[l], jnp.zeros((64, 256), F32)], axis=0).astype(BF16)
    rw["a0"] = c_a0[l].reshape(1, 256)
    rw["aup"] = jnp.concatenate([jnp.zeros((64, 256), F32), c_a_up[l]], axis=0).astype(BF16)
    rw["kk"] = c_k_k[l].reshape(1, 256)
    rw["ka"] = c_k_a[l].reshape(1, 256)
    rw["rk"] = c_r_k[l].reshape(1, 256)
    rw["gn"] = c_gn[l].reshape(1, 256)
    prm["rwkv"] = rw
    prm["vn"] = d_v_norm[l].reshape(1, 256)
    prm["ws"] = d_ws[l]
    prm["bias"] = jnp.repeat(d_b[l].T, HD, axis=1)
    prm["ws1"] = jnp.repeat(d_ws[l][:, 0, 0], HD).reshape(1, 256)
    prm["bias1"] = jnp.repeat(d_b[l][:, 0], HD).reshape(1, 256)
    wa = w_br_a[l].reshape(H, HD, D_MODEL)[jnp.array(A_HEAD_ORDER)].reshape(256, D_MODEL)
    prm["wbr"] = jnp.stack([wa, w_br_b[l], w_br_c[l], w_br_d[l]], axis=0).astype(BF16)
    prm["wout"] = w_out[l].astype(BF16)
    return prm


def kernel(x_prompt, x_sample, cache_a_k, cache_a_v, cache_b_latent, cache_b_kpe, state_c_wkv, state_c_shift, page_table, ln_g, w_in, a_q_norm, a_k_norm, b_q_a_norm, b_w_uq, b_q_norm, b_kv_a_norm, b_kpe_norm, b_w_uk, b_w_uv, c_mu, c_w0, c_w_up, c_a0, c_a_up, c_k_k, c_k_a, c_r_k, c_gn, d_v_norm, d_ws, d_b, w_br_a, w_br_b, w_br_c, w_br_d, w_out):
    depth = w_in.shape[0]
    bp, seq, _ = x_prompt.shape
    db, dec_t, _ = x_sample.shape
    n_pool = cache_a_k.shape[1]
    n_pages = page_table.shape[1]
    past = n_pages * PAGE
    assert dec_t == 1 and past % MOBA_BLOCK == 0 and past // MOBA_BLOCK >= MOBA_TOPK
    assert seq % (2 * MOBA_BLOCK) == 0 and seq // MOBA_BLOCK <= LANES
    n = bp * seq
    nblk = seq // MOBA_BLOCK

    ones_bd = jnp.asarray(np.kron(np.eye(4), np.ones((HD, HD))), BF16)
    pos_p = jnp.tile(jnp.arange(seq, dtype=jnp.int32), bp)
    pos_s = jnp.full((db,), past, jnp.int32)
    tabs_a_p, tabs_a_s = _rope_tables(pos_p, 0, HD), _rope_tables(pos_s, 0, HD)
    tabs_b_p, tabs_b_s = _rope_tables(pos_p, 64, ROPE_B), _rope_tables(pos_s, 64, ROPE_B)

    ck = jnp.transpose(cache_a_k, (0, 1, 3, 4, 2)).reshape(depth * n_pool, LANES, PAGE)
    cv = jnp.transpose(cache_a_v, (0, 1, 3, 4, 2)).reshape(depth * n_pool, LANES, PAGE)
    cl = cache_b_latent.reshape(depth * n_pool, PAGE, KV_LORA)
    cp = jnp.transpose(cache_b_kpe, (0, 1, 3, 2)).reshape(depth * n_pool, ROPE_B, PAGE)

    tm_prompt = min(1024, n)
    tm_prep = min(2048, seq)
    hp = x_prompt.reshape(n, D_MODEL)
    hs = x_sample.reshape(db, D_MODEL)
    outs = [[] for _ in range(13)]
    for l in range(depth):
        prm = _layer_params(l, a_q_norm, a_k_norm, b_q_a_norm, b_w_uq, b_q_norm, b_kv_a_norm, b_kpe_norm, b_w_uk,
                            b_w_uv, c_mu, c_w0, c_w_up, c_a0, c_a_up, c_k_k, c_k_a, c_r_k, c_gn, d_v_norm, d_ws, d_b,
                            w_br_a, w_br_b, w_br_c, w_br_d, w_out)
        w_packed = _pack_w_in(w_in[l])

        pp = _proj(hp, ln_g[l], w_packed, tm_prompt, 1280)
        qa, ka, kab, vab, kmean = _moba_prep(pp, prm["gq"], prm["gk"], tabs_a_p, ones_bd, tm_prep, seq)
        km_pad = jnp.zeros((bp, LANES, LANES), F32).at[:, :nblk].set(kmean.reshape(bp, nblk, LANES))
        oa = _moba_attn(qa, kab, vab, km_pad, bp, seq)
        qbb, kbb, vbb, lat, kpe = _mla_prep(pp, prm["gqa"], prm["wuq"], prm["gqb"], prm["gkv"], prm["gkpe"],
                                            prm["wuk"], prm["wuv"], tabs_b_p, tm_prep, False)
        ob = _mla_attn(qbb, kbb, vbb, bp, seq, min(512, seq))
        oc, wkv, last = _rwkv_seq(pp, prm["rwkv"], ones_bd, bp, seq, min(256, seq))
        (hp_new,) = _merge(hp, pp, oa, ob, oc, prm["vn"], prm["ws"], prm["bias"], prm["wbr"], prm["wout"],
                           min(256, n), CHUNK_D)
        outs[0].append(ka.reshape(bp, seq, 2, HD))
        outs[1].append(pp[:, P_A + 384:P_A + 512].reshape(bp, seq, 2, HD))
        outs[2].append(lat.reshape(bp, seq, KV_LORA))
        outs[3].append(kpe[:, 64:64 + ROPE_B].reshape(bp, seq, ROPE_B))
        wkv4 = wkv.reshape(bp, H, HD, H, HD)
        outs[4].append(jnp.stack([wkv4[:, h, :, h, :] for h in range(H)], axis=1))
        outs[5].append(last[:, 0, :C_SHIFT])
        hp = hp_new

        ps = _proj(hs, ln_g[l], w_packed, db, 1280)
        qa, ka, _, _ = _moba_prep(ps, prm["gq"], prm["gk"], tabs_a_s, ones_bd, db, 0)
        va = ps[:, P_A + 384:P_A + 512]
        q4 = qa.reshape(db, 2, 2, HD)
        zeros = jnp.zeros((db, HD), F32)
        q8 = jnp.stack([
            jnp.concatenate([q4[:, 0, 0], zeros], axis=1), jnp.concatenate([q4[:, 1, 0], zeros], axis=1),
            jnp.concatenate([zeros, q4[:, 1, 1]], axis=1), jnp.concatenate([zeros, q4[:, 0, 1]], axis=1),
        ] + [jnp.zeros((db, LANES), F32)] * 4, axis=1)
        idx = _moba_pick(page_table, q8, ck, l * n_pool)
        idx12 = idx[:, :H, :MOBA_TOPK].reshape(db, H * MOBA_TOPK)
        o8 = _moba_decode(idx12, page_table, q8, ka, va, ck, cv, l * n_pool)
        oa_s = jnp.concatenate([o8[:, 0, :HD], o8[:, 3, HD:], o8[:, 1, :HD], o8[:, 2, HD:]], axis=1)

        qlat, qpe, snew, lat_s, kpe_s = _mla_prep(ps, prm["gqa"], prm["wuq"], prm["gqb"], prm["gkv"], prm["gkpe"],
                                                  prm["wuk_t"], prm["wuv"], tabs_b_s, db, True)
        pad8 = lambda x: jnp.concatenate([x, jnp.zeros((db, 8 - H) + x.shape[2:], x.dtype)], axis=1)
        qlat8 = pad8(qlat.reshape(db, H, LANES))
        qpe8 = pad8(qpe.reshape(db, H, LANES)[:, :, 64:64 + ROPE_B])
        snew8 = pad8(snew[:, :H].reshape(db, H, 1))
        olat8 = _mla_decode(page_table, qlat8, qpe8, snew8, lat_s, cl, cp, l * n_pool)
        ob_s = _mla_up(olat8[:, :H].reshape(db, H * LANES), prm["wuv_heads"])

        prev = _lane_pad(state_c_shift[l], 0, 1024)
        oc_s, wkv_s = _rwkv_step(ps, prev, state_c_wkv[l], prm["rwkv"], ones_bd, 8)
        hs_new, v_rows = _merge(hs, ps, oa_s, ob_s, oc_s, prm["vn"], prm["ws1"], prm["bias1"], prm["wbr"],
                                prm["wout"], db, 1)
        outs[6].append(ka.reshape(db, 1, 2, HD))
        outs[7].append(va.reshape(db, 1, 2, HD))
        outs[8].append(lat_s.reshape(db, 1, KV_LORA))
        outs[9].append(kpe_s[:, 64:64 + ROPE_B].reshape(db, 1, ROPE_B))
        outs[10].append(wkv_s)
        outs[11].append(ps[:, P_C:P_C + C_SHIFT])
        outs[12].append(v_rows.reshape(db, 1, 256))
        hs = hs_new

    st = lambda xs: jnp.stack(xs, axis=0)
    return (hp.reshape(bp, seq, D_MODEL), hs.reshape(db, 1, D_MODEL)) + tuple(st(o) for o in outs)
```

```python
import functools

import numpy as np
import jax
import jax.numpy as jnp
from jax import lax
from jax.experimental import pallas as pl
from jax.experimental.pallas import tpu as pltpu

F32 = jnp.float32
BF16 = jnp.bfloat16
HIGHEST = lax.Precision.HIGHEST

D_MODEL = 1024
PAGE = 128
ROPE_THETA = 10000.0
NORM_EPS = 1e-6
GN_EPS = 64e-5
HD = 64
H = 4
W_BR = 256
MOBA_BLOCK = 256
MOBA_TOPK = 3
Q_LORA = 256
KV_LORA = 128
NOPE_B = 64
ROPE_B = 32
C_SHIFT = 896
CHUNK_D = 128
CHUNK_C = 64
NEG = -1e30
LANES = 128

P_GATES, P_A, P_B, P_C, P_D, P_COLS = 0, 1024, 1536, 2048, 3072, 3584
PROJ_TN = 1792
N_MERGE = 4 * D_MODEL
O_AQ, O_AK, O_AV, O_AG = 0, 256, 384, 512
O_BDQ, O_BDKV, O_BG = 768, 1024, 1184
O_CX, O_CG = 1440, 2336
O_DU, O_DV, O_DG, O_MG = 2592, 2848, 3104, 3360
A_HEAD_ORDER = (0, 3, 1, 2)

VMEM_LIMIT = 56 * 1024 * 1024


def _cparams(*sem):
    return pltpu.CompilerParams(dimension_semantics=sem, vmem_limit_bytes=VMEM_LIMIT)


def _dot(a, b, precision=None):
    return jnp.dot(a, b, preferred_element_type=F32, precision=precision)


def _dot_nt(a, b, precision=None):
    return lax.dot_general(a, b, (((1,), (1,)), ((), ())), preferred_element_type=F32, precision=precision)


def _dot_tn(a, b, precision=None):
    return lax.dot_general(a, b, (((0,), (0,)), ((), ())), preferred_element_type=F32, precision=precision)


def _seg_sum(x, ones):
    hi = x.astype(BF16)
    lo = (x - hi.astype(F32)).astype(BF16)
    return _dot(hi, ones) + _dot(lo, ones)


def _rope(x, cos, sin_a, sin_b, half):
    return x * cos + pltpu.roll(x, LANES - half, 1) * sin_a + pltpu.roll(x, half, 1) * sin_b


def _sigmoid(x):
    return 1.0 / (1.0 + jnp.exp(-x))


def _silu(x):
    return x * _sigmoid(x)


def _gelu(x):
    return 0.5 * x * (1.0 + jnp.tanh(np.sqrt(2.0 / np.pi).astype(np.float32) * (x + 0.044715 * (x * x * x))))


def _proj_kernel(x_ref, g_ref, w_ref, o_ref, h_ref):
    @pl.when(pl.program_id(1) == 0)
    def _():
        x = x_ref[...]
        y = x * lax.rsqrt(jnp.mean(x * x, axis=-1, keepdims=True) + NORM_EPS)
        h_ref[...] = (y * g_ref[...]).astype(BF16)

    o_ref[...] = _dot(h_ref[...], w_ref[...])


def _proj(x, g, w, tm, tn):
    n, d = x.shape
    c = w.shape[1]
    return pl.pallas_call(
        _proj_kernel,
        grid=(n // tm, c // tn),
        in_specs=[
            pl.BlockSpec((tm, d), lambda i, j: (i, 0)),
            pl.BlockSpec((1, d), lambda i, j: (0, 0)),
            pl.BlockSpec((d, tn), lambda i, j: (0, j)),
        ],
        out_specs=pl.BlockSpec((tm, tn), lambda i, j: (i, j)),
        out_shape=jax.ShapeDtypeStruct((n, c), F32),
        scratch_shapes=[pltpu.VMEM((tm, d), BF16)],
        compiler_params=_cparams("parallel", "arbitrary"),
        name="proj",
    )(x, g.reshape(1, d), w)


def _moba_prep_kernel(a_ref, gq_ref, gk_ref, cos_ref, sa_ref, sb_ref, ones_ref, q_ref, k_ref, kb_ref, vb_ref,
                      *km_ref, blocks, tiles_per_seq):
    a = a_ref[...]
    ones = ones_ref[...]
    cos, sa, sb = cos_ref[...], sa_ref[...], sb_ref[...]
    q, k, v = a[:, :256], a[:, 256:384], a[:, 384:512]
    qn = q * lax.rsqrt(_seg_sum(q * q, ones) * (1.0 / HD) + NORM_EPS) * gq_ref[...]
    kn = k * lax.rsqrt(_seg_sum(k * k, ones[:LANES, :LANES]) * (1.0 / HD) + NORM_EPS) * gk_ref[...]
    q_ref[:, :LANES] = _rope(qn[:, :LANES], cos, sa, sb, HD // 2)
    q_ref[:, LANES:] = _rope(qn[:, LANES:], cos, sa, sb, HD // 2)
    kr = _rope(kn, cos, sa, sb, HD // 2)
    k_ref[...] = kr
    kb_ref[:, :LANES] = kr.astype(BF16)
    vb_ref[...] = v.astype(BF16)
    if blocks:
        km_ref[0][...] = jnp.mean(kr.reshape(blocks, MOBA_BLOCK, LANES), axis=1)
        tm = a.shape[0]
        first_blk = (pl.program_id(0) % tiles_per_seq) * blocks
        blk = first_blk + (lax.broadcasted_iota(jnp.int32, (tm, LANES), 0) >> 8)
        lane = lax.broadcasted_iota(jnp.int32, (tm, LANES), 1)
        kb_ref[:, LANES:] = jnp.where(lane == blk, 1.0, 0.0).astype(BF16)
    else:
        kb_ref[:, LANES:] = jnp.zeros((a.shape[0], LANES), BF16)


def _moba_prep(p, gq, gk, tabs, ones, tm, seq):
    n = p.shape[0]
    row = lambda i: (i, 0)
    fix = lambda i: (0, 0)
    with_kmean = seq > 0
    assert MOBA_BLOCK == 256 and (not with_kmean or seq % tm == 0)
    blocks = tm // MOBA_BLOCK if with_kmean else 0
    tiles_per_seq = max(seq // tm, 1)
    tab = lambda i: (i % tiles_per_seq, 0)
    out_shape = [
        jax.ShapeDtypeStruct((n, 256), F32),
        jax.ShapeDtypeStruct((n, LANES), F32),
        jax.ShapeDtypeStruct((n, 256), BF16),
        jax.ShapeDtypeStruct((n, LANES), BF16),
    ]
    out_specs = [pl.BlockSpec((tm, 256), row), pl.BlockSpec((tm, LANES), row), pl.BlockSpec((tm, 256), row),
                 pl.BlockSpec((tm, LANES), row)]
    if with_kmean:
        out_shape.append(jax.ShapeDtypeStruct((n // MOBA_BLOCK, LANES), F32))
        out_specs.append(pl.BlockSpec((blocks, LANES), row))
    return pl.pallas_call(
        functools.partial(_moba_prep_kernel, blocks=blocks, tiles_per_seq=tiles_per_seq),
        grid=(n // tm,),
        in_specs=[
            pl.BlockSpec((tm, 512), lambda i: (i, P_A // 512)),
            pl.BlockSpec((1, 256), fix),
            pl.BlockSpec((1, LANES), fix),
            pl.BlockSpec((tm, LANES), tab),
            pl.BlockSpec((tm, LANES), tab),
            pl.BlockSpec((tm, LANES), tab),
            pl.BlockSpec((256, 256), fix),
        ],
        out_specs=out_specs,
        out_shape=out_shape,
        compiler_params=_cparams("parallel"),
        name="moba_prep",
    )(p, gq, gk, *tabs, ones)


def _exp_weights(s, m):
    p = jnp.exp((s - m).astype(BF16))
    t = p
    w = t.shape[1]
    while w > LANES:
        w //= 2
        t = t[:, :w] + t[:, w:]
    return p, jnp.sum(t.astype(F32), axis=-1, keepdims=True)


def _softmax_first(s, v):
    m = jnp.max(s, axis=-1, keepdims=True)
    p, ps = _exp_weights(s, m)
    return m, ps, _dot(p, v)


def _softmax_next(s, v, m, l, acc):
    mn = jnp.maximum(m, jnp.max(s, axis=-1, keepdims=True))
    alpha = jnp.exp(m - mn)
    p, ps = _exp_weights(s, mn)
    return mn, alpha * l + ps, alpha * acc + _dot(p, v)


def _moba_attn_kernel(q_ref, k_ref, v_ref, km_ref, o_ref):
    tq = q_ref.shape[0]
    i = pl.program_id(1)
    q = q_ref[...]
    km = km_ref[0]
    lane = lax.broadcasted_iota(jnp.int32, (tq, LANES), 1)
    lane_f = lane.astype(F32)
    lower = lane < HD

    stacked = []
    for plane in range(2):
        qp = q[:, plane * LANES:(plane + 1) * LANES]
        for half in range(2):
            qm = jnp.where(lower if half == 0 else jnp.logical_not(lower), qp, 0.0)
            gs = jnp.where(lane < i, _dot_nt(qm, km, HIGHEST), NEG)
            bias = jnp.where(lane == i, 0.0, NEG)
            for _ in range(MOBA_TOPK):
                m = jnp.max(gs, axis=1, keepdims=True)
                idx = jnp.min(jnp.where(gs == m, lane_f, float(LANES)), axis=1, keepdims=True)
                pick = jnp.logical_and(lane_f == idx, m > 0.5 * NEG)
                bias = jnp.where(pick, 0.0, bias)
                gs = jnp.where(pick, NEG, gs)
            stacked.append(jnp.concatenate([(qm * (HD ** -0.5)).astype(BF16), bias.astype(BF16)], axis=1))
    qa = jnp.concatenate(stacked, axis=0)

    tk = 2 * tq
    row = lax.broadcasted_iota(jnp.int32, (H * tq, tk), 0) & (tq - 1)
    col = lax.broadcasted_iota(jnp.int32, (H * tq, tk), 1)
    r0 = pl.multiple_of((i // 2) * tk, tk)
    s = jnp.where(col <= row + (i % 2) * tq, _dot_nt(qa, k_ref[pl.ds(r0, tk), :]), NEG)
    init = _softmax_first(s, v_ref[pl.ds(r0, tk), :])

    def body(j, carry):
        c0 = pl.multiple_of(j * tk, tk)
        return _softmax_next(_dot_nt(qa, k_ref[pl.ds(c0, tk), :]), v_ref[pl.ds(c0, tk), :], *carry)

    _, l, acc = lax.fori_loop(0, i // 2, body, init)
    o = acc / l
    o_ref[:, :LANES] = jnp.where(lower, o[0:tq], o[tq:2 * tq])
    o_ref[:, LANES:] = jnp.where(lower, o[2 * tq:3 * tq], o[3 * tq:4 * tq])


def _moba_attn(q, kb, vb, km, batch, seq):
    nq = seq // MOBA_BLOCK
    return pl.pallas_call(
        _moba_attn_kernel,
        grid=(batch, nq),
        in_specs=[
            pl.BlockSpec((MOBA_BLOCK, 256), lambda b, i: (b * nq + i, 0)),
            pl.BlockSpec((seq, 256), lambda b, i: (b, 0)),
            pl.BlockSpec((seq, LANES), lambda b, i: (b, 0)),
            pl.BlockSpec((1, LANES, LANES), lambda b, i: (b, 0, 0)),
        ],
        out_specs=pl.BlockSpec((MOBA_BLOCK, 256), lambda b, i: (b * nq + i, 0)),
        out_shape=jax.ShapeDtypeStruct((batch * seq, 256), F32),
        compiler_params=_cparams("parallel", "arbitrary"),
        name="moba_attn",
    )(q, kb, vb, km)


def _mla_prep_kernel(b_ref, gqa_ref, wuq_ref, gq_ref, gkv_ref, gkpe_ref, wuk_ref, wuv_ref, cos_ref, sa_ref, sb_ref,
                     *out_refs, decode):
    x = b_ref[...]
    cos, sa, sb = cos_ref[...], sa_ref[...], sb_ref[...]
    dq, lat_raw, kpe_raw = x[:, :256], x[:, 256:384], x[:, 384:512]
    cq = dq * lax.rsqrt(jnp.mean(dq * dq, axis=-1, keepdims=True) + NORM_EPS) * gqa_ref[...]
    q = _dot(cq.astype(BF16), wuq_ref[...])
    lat = lat_raw * lax.rsqrt(jnp.mean(lat_raw * lat_raw, axis=-1, keepdims=True) + NORM_EPS) * gkv_ref[...]
    kpe = kpe_raw * lax.rsqrt(jnp.sum(kpe_raw * kpe_raw, axis=-1, keepdims=True) * (1.0 / ROPE_B) + NORM_EPS)
    kpe = _rope(kpe * gkpe_ref[...], cos, sa, sb, ROPE_B // 2)
    scale = (NOPE_B + ROPE_B) ** -0.5
    qs = []
    for h in range(H):
        qh = q[:, h * LANES:(h + 1) * LANES]
        qh = qh * lax.rsqrt(jnp.sum(qh * qh, axis=-1, keepdims=True) * (1.0 / (NOPE_B + ROPE_B)) + NORM_EPS)
        qs.append(_rope(qh * gq_ref[...], cos, sa, sb, ROPE_B // 2) * scale)
    latb = lat.astype(BF16)
    if decode:
        qlat_ref, qpe_ref, snew_ref, lat_ref, kpe_ref = out_refs
        lane = lax.broadcasted_iota(jnp.int32, kpe.shape, 1)
        snew = jnp.zeros(kpe.shape, F32)
        for h in range(H):
            ql = _dot(qs[h].astype(BF16), wuk_ref[h])
            qlat_ref[:, h * LANES:(h + 1) * LANES] = ql
            qpe_ref[:, h * LANES:(h + 1) * LANES] = qs[h]
            s = jnp.sum(ql.astype(BF16).astype(F32) * latb.astype(F32), axis=-1, keepdims=True)
            s = s + jnp.sum(qs[h].astype(BF16).astype(F32) * kpe.astype(BF16).astype(F32), axis=-1, keepdims=True)
            snew = jnp.where(lane == h, s, snew)
        snew_ref[...] = snew
    else:
        qb_ref, kb_ref, vb_ref, lat_ref, kpe_ref = out_refs
        kn = _dot(latb, wuk_ref[...])
        for h in range(H):
            qb_ref[:, h * LANES:(h + 1) * LANES] = qs[h].astype(BF16)
            kb_ref[:, h * LANES:(h + 1) * LANES] = (kn[:, h * LANES:(h + 1) * LANES] + kpe).astype(BF16)
        vb_ref[...] = _dot(latb, wuv_ref[...]).astype(BF16)
    lat_ref[...] = lat
    kpe_ref[...] = kpe


def _mla_prep(p, gqa, wuq, gq, gkv, gkpe, wuk, wuv, tabs, tm, decode):
    n = p.shape[0]
    row = lambda i: (i, 0)
    fix = lambda i: (0, 0)
    tiles_per_tab = tabs[0].shape[0] // tm
    tab = lambda i: (i % tiles_per_tab, 0)
    wide = lambda dt: (jax.ShapeDtypeStruct((n, 512), dt), pl.BlockSpec((tm, 512), row))
    plane = lambda dt: (jax.ShapeDtypeStruct((n, LANES), dt), pl.BlockSpec((tm, LANES), row))
    if decode:
        outs = [wide(F32), wide(F32), plane(F32), plane(F32), plane(F32)]
        wuk_spec = pl.BlockSpec((H, LANES, LANES), lambda i: (0, 0, 0))
    else:
        outs = [wide(BF16), wide(BF16), (jax.ShapeDtypeStruct((n, 256), BF16), pl.BlockSpec((tm, 256), row)),
                plane(F32), plane(F32)]
        wuk_spec = pl.BlockSpec((LANES, 512), fix)
    return pl.pallas_call(
        functools.partial(_mla_prep_kernel, decode=decode),
        grid=(n // tm,),
        in_specs=[
            pl.BlockSpec((tm, 512), lambda i: (i, P_B // 512)),
            pl.BlockSpec((1, 256), fix),
            pl.BlockSpec((256, 512), fix),
            pl.BlockSpec((1, LANES), fix),
            pl.BlockSpec((1, LANES), fix),
            pl.BlockSpec((1, LANES), fix),
            wuk_spec,
            pl.BlockSpec((LANES, 256), fix),
            pl.BlockSpec((tm, LANES), tab),
            pl.BlockSpec((tm, LANES), tab),
            pl.BlockSpec((tm, LANES), tab),
        ],
        out_specs=[s for _, s in outs],
        out_shape=[s for s, _ in outs],
        compiler_params=_cparams("parallel"),
        name="mla_prep",
    )(p, gqa, wuq, gq, gkv, gkpe, wuk, wuv, *tabs)


def _mla_attn_kernel(q_ref, k_ref, v_ref, o_ref):
    tq = q_ref.shape[0]
    i = pl.program_id(1)
    lane = lax.broadcasted_iota(jnp.int32, (tq, LANES), 1)
    lower = lane < HD
    row = lax.broadcasted_iota(jnp.int32, (tq, tq), 0)
    col = lax.broadcasted_iota(jnp.int32, (tq, tq), 1)
    r0 = pl.multiple_of(i * tq, tq)
    qs = [q_ref[:, h * LANES:(h + 1) * LANES] for h in range(H)]

    def tile(c0, h):
        return (k_ref[pl.ds(c0, tq), h * LANES:(h + 1) * LANES],
                v_ref[pl.ds(c0, tq), (h // 2) * LANES:(h // 2 + 1) * LANES])

    init = []
    for h in range(H):
        kh, vh = tile(r0, h)
        init.extend(_softmax_first(jnp.where(col <= row, _dot_nt(qs[h], kh), NEG), vh))

    def body(j, carry):
        c0 = pl.multiple_of(j * tq, tq)
        out = []
        for h in range(H):
            kh, vh = tile(c0, h)
            out.extend(_softmax_next(_dot_nt(qs[h], kh), vh, *carry[3 * h:3 * h + 3]))
        return tuple(out)

    fin = lax.fori_loop(0, i, body, tuple(init))
    for plane in range(2):
        o0 = fin[6 * plane + 2] / fin[6 * plane + 1]
        o1 = fin[6 * plane + 5] / fin[6 * plane + 4]
        o_ref[:, plane * LANES:(plane + 1) * LANES] = jnp.where(lower, o0, o1)


def _mla_attn(qb, kb, vb, batch, seq, tq):
    nq = seq // tq
    return pl.pallas_call(
        _mla_attn_kernel,
        grid=(batch, nq),
        in_specs=[
            pl.BlockSpec((tq, 512), lambda b, i: (b * nq + i, 0)),
            pl.BlockSpec((seq, 512), lambda b, i: (b, 0)),
            pl.BlockSpec((seq, 256), lambda b, i: (b, 0)),
        ],
        out_specs=pl.BlockSpec((tq, 256), lambda b, i: (b * nq + i, 0)),
        out_shape=jax.ShapeDtypeStruct((batch * seq, 256), F32),
        compiler_params=_cparams("parallel", "arbitrary"),
        name="mla_attn",
    )(qb, kb, vb)


def _rwkv_features(xc, prev, mu, w0, wup, a0, aup, kk_w, ka_w, ones):
    xs = xc + (prev - xc) * mu
    r, k, v, lora = xs[:, :256], xs[:, 256:512], xs[:, 512:768], xs[:, 768:896]
    w_pre = w0 + _dot(jnp.tanh(lora).astype(BF16), wup)
    z = -w_pre
    w_log = -(jnp.maximum(z, 0.0) + jnp.log(1.0 + jnp.exp(-jnp.abs(z)))) - 0.5
    lw = -jnp.exp(w_log)
    a = _sigmoid(a0 + _dot(lora.astype(BF16), aup))
    kk = k * kk_w
    kk = kk * lax.rsqrt(_seg_sum(kk * kk, ones) + 1e-12)
    k2 = k * (1.0 + (a - 1.0) * ka_w)
    return r, k2, v, kk, kk * a, lw


def _rwkv_out(y, r, k2, v, rk_w, gn_w, ones):
    mean = _seg_sum(y, ones) * (1.0 / HD)
    d = y - mean
    var = _seg_sum(d * d, ones) * (1.0 / HD)
    return d * lax.rsqrt(var + GN_EPS) * gn_w + _seg_sum(r * k2 * rk_w, ones) * v


def _block_stack(x, same_blk):
    return jnp.where(same_blk, jnp.concatenate([x, x, x, x], axis=0), 0.0)


def _row_blocks_sum(x):
    c = CHUNK_C
    return x[0:c] + x[c:2 * c] + x[2 * c:3 * c] + x[3 * c:4 * c]


def _rwkv_seq_kernel(x_ref, mu_ref, w0_ref, wup_ref, a0_ref, aup_ref, kk_ref, ka_ref, rk_ref, gn_ref, ones_ref,
                     o_ref, s_out_ref, last_ref, s_ref, prev_ref):
    tt = x_ref.shape[0]
    c = CHUNK_C

    @pl.when(pl.program_id(1) == 0)
    def _():
        s_ref[...] = jnp.zeros(s_ref.shape, F32)
        prev_ref[...] = jnp.zeros(prev_ref.shape, F32)

    ones = ones_ref[...]
    xc = x_ref[...]
    rid = lax.broadcasted_iota(jnp.int32, xc.shape, 0)
    prev = jnp.where(rid == 0, prev_ref[...], pltpu.roll(xc, 1, 0))
    prev_ref[...] = xc[tt - 1:tt, :]
    last_ref[0] = xc[tt - 1:tt, :]
    r, k2, v, kk, bv, lw = _rwkv_features(xc, prev, mu_ref[...], w0_ref[...], wup_ref[...], a0_ref[...], aup_ref[...],
                                          kk_ref[...], ka_ref[...], ones)

    rr = lax.broadcasted_iota(jnp.int32, (256, 256), 0)
    cc = lax.broadcasted_iota(jnp.int32, (256, 256), 1)
    same_blk = (rr >> 6) == (cc >> 6)
    strict = (rr & 63) > (cc & 63)
    incl = (rr & 63) >= (cc & 63)
    eye = jnp.where(rr == cc, 1.0, 0.0)
    tri = jnp.where(incl[:c, :c], 1.0, 0.0)

    chunks = []
    for ci in range(tt // c):
        sl = slice(ci * c, (ci + 1) * c)
        lw_c, r_c, k_c, v_c, kk_c, bv_c = lw[sl], r[sl], k2[sl], v[sl], kk[sl], bv[sl]
        cum = _dot(tri, lw_c, HIGHEST)
        last = cum[c - 1:c, :]
        e_neg = jnp.exp(-cum)
        e_end = jnp.exp(last - cum)
        chunks.append(dict(
            a=_block_stack(-kk_c * jnp.exp(cum - lw_c), same_blk).astype(BF16),
            r=_block_stack(r_c * jnp.exp(cum), same_blk).astype(BF16),
            b=_block_stack(bv_c * e_neg, same_blk).astype(BF16),
            k=_block_stack(k_c * e_neg, same_blk).astype(BF16),
            v=_block_stack(v_c, same_blk).astype(BF16),
            b_end=_block_stack(bv_c * e_end, same_blk).astype(BF16),
            k_end=_block_stack(k_c * e_end, same_blk).astype(BF16), decay=jnp.exp(last)))
    for ch in chunks:
        ch["n_ab"] = jnp.where(strict, _dot_nt(ch["a"], ch["b"]), 0.0)
        ch["n_ak"] = jnp.where(strict, _dot_nt(ch["a"], ch["k"]), 0.0).astype(BF16)
        ch["n_rb"] = jnp.where(incl, _dot_nt(ch["r"], ch["b"]), 0.0).astype(BF16)
        ch["n_rk"] = jnp.where(incl, _dot_nt(ch["r"], ch["k"]), 0.0).astype(BF16)
        ch["t_inv"] = eye + ch["n_ab"]
        ch["pw"] = ch["n_ab"].astype(BF16)
    for _ in range(5):
        for ch in chunks:
            ch["pw"] = _dot(ch["pw"], ch["pw"]).astype(BF16)
        for ch in chunks:
            ch["t_inv"] = ch["t_inv"] + _dot(ch["t_inv"].astype(BF16), ch["pw"])
    for ch in chunks:
        t_inv = ch["t_inv"].astype(BF16)
        akv = _dot(ch["n_ak"], ch["v"]).astype(BF16)
        ch["w"] = _dot(t_inv, ch["a"]).astype(BF16)
        ch["u0"] = _dot(t_inv, akv).astype(BF16)
    for ch in chunks:
        ch["r2"] = (ch["r"].astype(F32) + _dot(ch["n_rb"], ch["w"])).astype(BF16)
        ch["y0"] = _dot(ch["n_rb"], ch["u0"]) + _dot(ch["n_rk"], ch["v"])
        ch["m"] = _dot_tn(ch["w"], ch["b_end"]).astype(BF16)
        ch["g"] = _dot_tn(ch["u0"], ch["b_end"]) + _dot_tn(ch["v"], ch["k_end"])

    s = s_ref[...]
    ys = []
    for ch in chunks:
        sb = s.astype(BF16)
        ys.append(_row_blocks_sum(_dot_nt(ch["r2"], sb) + ch["y0"]))
        s = s * ch["decay"] + _dot(sb, ch["m"]) + ch["g"]
    s_ref[...] = s
    s_out_ref[0] = s
    y = jnp.concatenate(ys, axis=0)
    o_ref[...] = _rwkv_out(y, r, k2, v, rk_ref[...], gn_ref[...], ones)


def _rwkv_seq(p, prm, ones, batch, seq, tt):
    nt = seq // tt
    fix = lambda b, t: (0, 0)
    vec256 = pl.BlockSpec((1, 256), fix)
    return pl.pallas_call(
        _rwkv_seq_kernel,
        grid=(batch, nt),
        in_specs=[
            pl.BlockSpec((tt, 1024), lambda b, t: (b * nt + t, P_C // 1024)),
            pl.BlockSpec((1, 1024), fix),
            vec256, pl.BlockSpec((LANES, 256), fix), vec256, pl.BlockSpec((LANES, 256), fix),
            vec256, vec256, vec256, vec256,
            pl.BlockSpec((256, 256), fix),
        ],
        out_specs=[
            pl.BlockSpec((tt, 256), lambda b, t: (b * nt + t, 0)),
            pl.BlockSpec((1, 256, 256), lambda b, t: (b, 0, 0)),
            pl.BlockSpec((1, 1, 1024), lambda b, t: (b, 0, 0)),
        ],
        out_shape=[
            jax.ShapeDtypeStruct((batch * seq, 256), F32),
            jax.ShapeDtypeStruct((batch, 256, 256), F32),
            jax.ShapeDtypeStruct((batch, 1, 1024), F32),
        ],
        scratch_shapes=[pltpu.VMEM((256, 256), F32), pltpu.VMEM((1, 1024), F32)],
        compiler_params=_cparams("parallel", "arbitrary"),
        name="rwkv_seq",
    )(p, prm["mu"], prm["w0"], prm["wup"], prm["a0"], prm["aup"], prm["kk"], prm["ka"], prm["rk"], prm["gn"], ones)


def _rwkv_step_kernel(x_ref, prev_ref, s_ref, mu_ref, w0_ref, wup_ref, a0_ref, aup_ref, kk_ref, ka_ref, rk_ref,
                      gn_ref, ones_ref, o_ref, s_out_ref, yt_ref):
    nb = x_ref.shape[0]
    ones = ones_ref[...]
    r, k2, v, kk, bv, lw = _rwkv_features(x_ref[...], prev_ref[...], mu_ref[...], w0_ref[...], wup_ref[...],
                                          a0_ref[...], aup_ref[...], kk_ref[...], ka_ref[...], ones)
    w = jnp.exp(lw)
    vt = v.T
    for b in range(nb):
        for h in range(H):
            hs = slice(h * HD, (h + 1) * HD)
            st = s_ref[b, h]
            sa = -jnp.sum(st * kk[b:b + 1, hs], axis=-1, keepdims=True)
            st = st * w[b:b + 1, hs] + sa * bv[b:b + 1, hs] + vt[hs, b:b + 1] * k2[b:b + 1, hs]
            s_out_ref[b, h] = st
            yt_ref[hs, b:b + 1] = jnp.sum(st * r[b:b + 1, hs], axis=-1, keepdims=True)
    y = yt_ref[...].T
    o_ref[...] = _rwkv_out(y, r, k2, v, rk_ref[...], gn_ref[...], ones)


def _rwkv_step(p, prev, state, prm, ones, nb):
    n = p.shape[0]
    fix = lambda i: (0, 0)
    vec256 = pl.BlockSpec((1, 256), fix)
    return pl.pallas_call(
        _rwkv_step_kernel,
        grid=(n // nb,),
        in_specs=[
            pl.BlockSpec((nb, 1024), lambda i: (i, P_C // 1024)),
            pl.BlockSpec((nb, 1024), lambda i: (i, 0)),
            pl.BlockSpec((nb, H, HD, HD), lambda i: (i, 0, 0, 0)),
            pl.BlockSpec((1, 1024), fix),
            vec256, pl.BlockSpec((LANES, 256), fix), vec256, pl.BlockSpec((LANES, 256), fix),
            vec256, vec256, vec256, vec256,
            pl.BlockSpec((256, 256), fix),
        ],
        out_specs=[
            pl.BlockSpec((nb, 256), lambda i: (i, 0)),
            pl.BlockSpec((nb, H, HD, HD), lambda i: (i, 0, 0, 0)),
        ],
        out_shape=[
            jax.ShapeDtypeStruct((n, 256), F32),
            jax.ShapeDtypeStruct((n, H, HD, HD), F32),
        ],
        scratch_shapes=[pltpu.VMEM((256, nb), F32)],
        compiler_params=_cparams("parallel"),
        name="rwkv_step",
    )(p, prev, state, prm["mu"], prm["w0"], prm["wup"], prm["a0"], prm["aup"], prm["kk"], prm["ka"], prm["rk"],
      prm["gn"], ones)


def _merge_kernel(x_ref, lng_ref, wmg_ref, g_ref, oa_ref, ob_ref, oc_ref, d_ref, vn_ref, ws_ref, bias_ref, wbr_ref,
                  wout_ref, o_ref, *v_ref, chunk):
    tm = x_ref.shape[0]
    x = x_ref[...]
    hb = (x * lax.rsqrt(jnp.mean(x * x, axis=-1, keepdims=True) + NORM_EPS) * lng_ref[...]).astype(BF16)
    d = d_ref[...]
    u = _gelu(d[:, :256])
    vg = _gelu(d[:, 256:512])
    v = vg * lax.rsqrt(jnp.mean(vg * vg, axis=-1, keepdims=True) + NORM_EPS) * vn_ref[...]
    if chunk == 1:
        sp = v * ws_ref[...] + bias_ref[...]
        v_ref[0][...] = v
    else:
        rr = lax.broadcasted_iota(jnp.int32, (chunk, chunk), 0)
        cc = lax.broadcasted_iota(jnp.int32, (chunk, chunk), 1)
        grp = lax.broadcasted_iota(jnp.int32, (chunk, 256), 1) >> 6
        vb = v.astype(BF16)
        ws = [jnp.where(cc <= rr, ws_ref[g], 0.0).astype(BF16) for g in range(H)]
        parts = []
        for ci in range(tm // chunk):
            vc = vb[ci * chunk:(ci + 1) * chunk]
            acc = bias_ref[...]
            for g in range(H):
                acc = acc + jnp.where(grp == g, _dot(ws[g], vc), 0.0)
            parts.append(acc)
        sp = jnp.concatenate(parts, axis=0)
    od = u * sp
    gates = g_ref[...]
    y = jnp.zeros((tm, D_MODEL), F32)
    for bi, o in enumerate((oa_ref[...], ob_ref[...], oc_ref[...], od)):
        z = (o * _silu(gates[:, bi * 256:(bi + 1) * 256])).astype(BF16)
        mg = _dot(hb, wmg_ref[:, bi * D_MODEL:(bi + 1) * D_MODEL])
        y = y + _sigmoid(mg) * _dot(z, wbr_ref[bi])
    o_ref[...] = x + _dot(y.astype(BF16), wout_ref[...])


def _merge(x, lng, wmg, p, oa, ob, oc, vn, ws, bias, wbr, wout, tm, chunk):
    n = x.shape[0]
    row = lambda i: (i, 0)
    fix = lambda i: (0, 0)
    out_shape = [jax.ShapeDtypeStruct((n, D_MODEL), F32)]
    out_specs = [pl.BlockSpec((tm, D_MODEL), row)]
    if chunk == 1:
        out_shape.append(jax.ShapeDtypeStruct((n, 256), F32))
        out_specs.append(pl.BlockSpec((tm, 256), row))
        ws_spec = pl.BlockSpec((1, 256), fix)
        bias_spec = pl.BlockSpec((1, 256), fix)
    else:
        ws_spec = pl.BlockSpec((H, chunk, chunk), lambda i: (0, 0, 0))
        bias_spec = pl.BlockSpec((chunk, 256), fix)
    return pl.pallas_call(
        functools.partial(_merge_kernel, chunk=chunk),
        grid=(n // tm,),
        in_specs=[
            pl.BlockSpec((tm, D_MODEL), row),
            pl.BlockSpec((1, D_MODEL), fix),
            pl.BlockSpec((D_MODEL, N_MERGE), fix),
            pl.BlockSpec((tm, 1024), lambda i: (i, P_GATES // 1024)),
            pl.BlockSpec((tm, 256), row),
            pl.BlockSpec((tm, 256), row),
            pl.BlockSpec((tm, 256), row),
            pl.BlockSpec((tm, 512), lambda i: (i, P_D // 512)),
            pl.BlockSpec((1, 256), fix),
            ws_spec,
            bias_spec,
            pl.BlockSpec((H, 256, D_MODEL), lambda i: (0, 0, 0)),
            pl.BlockSpec((D_MODEL, D_MODEL), fix),
        ],
        out_specs=out_specs,
        out_shape=out_shape,
        compiler_params=_cparams("parallel"),
        name="merge",
    )(x, lng.reshape(1, D_MODEL), wmg, p, oa, ob, oc, p, vn, ws, bias, wbr, wout)


def _page_pipeline(b, n_rows, copies_of, n_copies):
    slot = b % 2

    @pl.when(b == 0)
    def _():
        for i in range(n_copies):
            for cp in copies_of(0, 0, i):
                cp.start()

    for i in range(n_copies):
        for cp in copies_of(b, slot, i):
            cp.wait()
    nxt = jnp.minimum(b + 1, n_rows - 1)
    for i in range(n_copies):
        for cp in copies_of(nxt, 1 - slot, i):
            cp.start()
    return slot


def _page_drain(b, n_rows, copies_of, n_copies):
    @pl.when(b == n_rows - 1)
    def _():
        for i in range(n_copies):
            for cp in copies_of(b, 1 - b % 2, i):
                cp.wait()


def _moba_pick_kernel(pt_ref, q_ref, k_hbm, idx_ref, k_buf, sem, *, base, n_pages):
    b = pl.program_id(0)

    def copies_of(row, sl, i):
        return (pltpu.make_async_copy(k_hbm.at[base + pt_ref[row, i]], k_buf.at[sl, :, pl.ds(i * PAGE, PAGE)],
                                      sem.at[sl]),)

    slot = _page_pipeline(b, pl.num_programs(0), copies_of, n_pages)
    nblk = n_pages * PAGE // MOBA_BLOCK
    q = q_ref[0]
    hi = q.astype(BF16).astype(F32)
    q2 = jnp.concatenate([hi, q - hi], axis=0).astype(BF16)
    s2 = _dot(q2, k_buf[slot].astype(BF16))
    s = s2[:8] + s2[8:]
    lane = lax.broadcasted_iota(jnp.int32, (8, LANES), 1)
    lane_f = lane.astype(F32)
    gs = jnp.full((8, LANES), NEG, F32)
    for blk in range(nblk):
        tot = jnp.sum(s[:, blk * MOBA_BLOCK:(blk + 1) * MOBA_BLOCK], axis=-1, keepdims=True)
        gs = jnp.where(lane == blk, tot * (1.0 / MOBA_BLOCK), gs)
    out = jnp.zeros((8, LANES), jnp.int32)
    for t in range(MOBA_TOPK):
        m = jnp.max(gs, axis=1, keepdims=True)
        idx = jnp.min(jnp.where(gs == m, lane_f, float(LANES)), axis=1, keepdims=True)
        out = jnp.where(lane == t, idx.astype(jnp.int32), out)
        gs = jnp.where(lane_f == idx, NEG, gs)
    idx_ref[0] = out
    _page_drain(b, pl.num_programs(0), copies_of, n_pages)


def _moba_pick(page_table, q8, cache_kt, layer_base):
    nb, n_pages = page_table.shape
    assert n_pages * PAGE // MOBA_BLOCK <= LANES
    grid_spec = pltpu.PrefetchScalarGridSpec(
        num_scalar_prefetch=1,
        grid=(nb,),
        in_specs=[pl.BlockSpec((1, 8, LANES), lambda b, pt: (b, 0, 0)), pl.BlockSpec(memory_space=pl.ANY)],
        out_specs=pl.BlockSpec((1, 8, LANES), lambda b, pt: (b, 0, 0)),
        scratch_shapes=[pltpu.VMEM((2, LANES, n_pages * PAGE), F32), pltpu.SemaphoreType.DMA((2,))],
    )
    return pl.pallas_call(
        functools.partial(_moba_pick_kernel, base=layer_base, n_pages=n_pages),
        grid_spec=grid_spec,
        out_shape=jax.ShapeDtypeStruct((nb, 8, LANES), jnp.int32),
        compiler_params=_cparams("arbitrary"),
        name="moba_pick",
    )(page_table, q8, cache_kt)


def _moba_decode_kernel(idx_ref, pt_ref, q_ref, kn_ref, vn_ref, k_hbm, v_hbm, o_ref, k_buf, v_buf, sem, *, base):
    b = pl.program_id(0)
    per_blk = MOBA_BLOCK // PAGE
    per_head = MOBA_TOPK * per_blk

    def copies_of(row, sl, i):
        out = []
        for h in range(H):
            for t in range(MOBA_TOPK):
                for pi in range(per_blk):
                    page = base + pt_ref[row, idx_ref[row, h * MOBA_TOPK + t] * per_blk + pi]
                    dst = pl.ds((t * per_blk + pi) * PAGE, PAGE)
                    out.append(pltpu.make_async_copy(k_hbm.at[page], k_buf.at[sl, h, :, dst], sem.at[0, sl]))
                    out.append(pltpu.make_async_copy(v_hbm.at[page], v_buf.at[sl, h, :, dst], sem.at[1, sl]))
        return out

    slot = _page_pipeline(b, pl.num_programs(0), copies_of, 1)
    q = q_ref[0]
    qb = (q * (HD ** -0.5)).astype(BF16)
    kn = kn_ref[0].astype(BF16).astype(F32)
    vn = vn_ref[0].astype(BF16).astype(F32)
    s_own = jnp.sum(qb.astype(F32) * kn, axis=-1, keepdims=True)
    rowid = lax.broadcasted_iota(jnp.int32, (8, LANES), 0)
    out = jnp.zeros((8, LANES), F32)
    for h in range(H):
        s = _dot(qb, k_buf[slot, h].astype(BF16))
        m = jnp.maximum(s_own, jnp.max(s, axis=-1, keepdims=True))
        p_own = jnp.exp(s_own - m)
        p = jnp.exp(s - m)
        l = p_own + jnp.sum(p, axis=-1, keepdims=True)
        acc = p_own.astype(BF16).astype(F32) * vn + _dot_nt(p.astype(BF16), v_buf[slot, h].astype(BF16))
        out = jnp.where(rowid == h, acc / l, out)
    o_ref[0] = out
    _page_drain(b, pl.num_programs(0), copies_of, 1)


def _moba_decode(idx, page_table, q8, k_new, v_new, cache_kt, cache_vt, layer_base):
    nb = page_table.shape[0]
    sel_tokens = MOBA_TOPK * MOBA_BLOCK
    vec = pl.BlockSpec((1, 1, LANES), lambda b, idx_r, pt: (b, 0, 0))
    grid_spec = pltpu.PrefetchScalarGridSpec(
        num_scalar_prefetch=2,
        grid=(nb,),
        in_specs=[pl.BlockSpec((1, 8, LANES), lambda b, idx_r, pt: (b, 0, 0)), vec, vec,
                  pl.BlockSpec(memory_space=pl.ANY), pl.BlockSpec(memory_space=pl.ANY)],
        out_specs=pl.BlockSpec((1, 8, LANES), lambda b, idx_r, pt: (b, 0, 0)),
        scratch_shapes=[pltpu.VMEM((2, H, LANES, sel_tokens), F32), pltpu.VMEM((2, H, LANES, sel_tokens), F32),
                        pltpu.SemaphoreType.DMA((2, 2))],
    )
    return pl.pallas_call(
        functools.partial(_moba_decode_kernel, base=layer_base),
        grid_spec=grid_spec,
        out_shape=jax.ShapeDtypeStruct((nb, 8, LANES), F32),
        compiler_params=_cparams("arbitrary"),
        name="moba_decode",
    )(idx, page_table, q8, k_new.reshape(nb, 1, LANES), v_new.reshape(nb, 1, LANES), cache_kt, cache_vt)


def _mla_decode_kernel(pt_ref, ql_ref, qp_ref, sn_ref, ln_ref, lat_hbm, kpe_hbm, o_ref, lat_buf, kpe_buf, sem, *,
                       base, n_pages):
    b = pl.program_id(0)

    def copies_of(row, sl, i):
        page = base + pt_ref[row, i]
        off = pl.ds(i * PAGE, PAGE)
        return (pltpu.make_async_copy(lat_hbm.at[page], lat_buf.at[sl, off, :], sem.at[0, sl]),
                pltpu.make_async_copy(kpe_hbm.at[page], kpe_buf.at[sl, :, off], sem.at[1, sl]))

    slot = _page_pipeline(b, pl.num_programs(0), copies_of, n_pages)
    ql = ql_ref[0].astype(BF16)
    qp = qp_ref[0].astype(BF16)
    lat = lat_buf[slot].astype(BF16)
    s = _dot_nt(ql, lat) + _dot(qp, kpe_buf[slot].astype(BF16))
    s_new = sn_ref[0]
    m = jnp.maximum(s_new, jnp.max(s, axis=-1, keepdims=True))
    p_new = jnp.exp(s_new - m)
    p = jnp.exp(s - m).astype(BF16)
    l = p_new + jnp.sum(p.astype(F32), axis=-1, keepdims=True)
    acc = p_new.astype(BF16).astype(F32) * ln_ref[0].astype(BF16).astype(F32) + _dot(p, lat)
    o_ref[0] = acc / l
    _page_drain(b, pl.num_programs(0), copies_of, n_pages)


def _mla_decode(page_table, qlat8, qpe8, snew8, lat_new, cache_lat, cache_kpet, layer_base):
    nb, n_pages = page_table.shape
    per_b = lambda shape: pl.BlockSpec((1,) + shape, lambda b, pt: (b, 0, 0))
    grid_spec = pltpu.PrefetchScalarGridSpec(
        num_scalar_prefetch=1,
        grid=(nb,),
        in_specs=[per_b((8, LANES)), per_b((8, ROPE_B)), per_b((8, 1)), per_b((1, LANES)),
                  pl.BlockSpec(memory_space=pl.ANY), pl.BlockSpec(memory_space=pl.ANY)],
        out_specs=per_b((8, LANES)),
        scratch_shapes=[pltpu.VMEM((2, n_pages * PAGE, KV_LORA), F32), pltpu.VMEM((2, ROPE_B, n_pages * PAGE), F32),
                        pltpu.SemaphoreType.DMA((2, 2))],
    )
    return pl.pallas_call(
        functools.partial(_mla_decode_kernel, base=layer_base, n_pages=n_pages),
        grid_spec=grid_spec,
        out_shape=jax.ShapeDtypeStruct((nb, 8, LANES), F32),
        compiler_params=_cparams("arbitrary"),
        name="mla_decode",
    )(page_table, qlat8, qpe8, snew8, lat_new.reshape(nb, 1, LANES), cache_lat, cache_kpet)


def _mla_up_kernel(o_ref, w_ref, out_ref):
    acc = jnp.zeros(out_ref.shape, F32)
    for h in range(H):
        acc = acc + _dot(o_ref[:, h * LANES:(h + 1) * LANES].astype(BF16), w_ref[h])
    out_ref[...] = acc


def _mla_up(o_lat, wuv_heads):
    n = o_lat.shape[0]
    return pl.pallas_call(
        _mla_up_kernel,
        out_shape=jax.ShapeDtypeStruct((n, 256), F32),
        name="mla_up",
    )(o_lat, wuv_heads)


def _pack_w_in(w):
    d = w.shape[0]
    z = lambda n: jnp.zeros((d, n), w.dtype)
    col = lambda o, n: w[:, o:o + n]
    aq = jnp.concatenate([col(O_AQ + h * HD, HD) for h in A_HEAD_ORDER], axis=1)
    ag = jnp.concatenate([col(O_AG + h * HD, HD) for h in A_HEAD_ORDER], axis=1)
    parts = [
        ag, col(O_BG, 256), col(O_CG, 256), col(O_DG, 256),
        aq, col(O_AK, 128), col(O_AV, 128),
        col(O_BDQ, 256), col(O_BDKV, KV_LORA), z(64), col(O_BDKV + KV_LORA, ROPE_B), z(32),
        col(O_CX, C_SHIFT), z(128),
        col(O_DU, 256), col(O_DV, 256),
    ]
    out = jnp.concatenate(parts, axis=1).astype(BF16)
    assert out.shape[1] == P_COLS
    return out, col(O_MG, N_MERGE).astype(BF16)


def _rope_tables(pos, lane_lo, dim):
    half = dim // 2
    inv = ROPE_THETA ** (-jnp.arange(half, dtype=F32) / half)
    ang = pos.astype(F32)[:, None] * inv[None, :]
    cos, sin = jnp.cos(ang), jnp.sin(ang)
    lane = np.arange(LANES)
    span = LANES if dim == HD else dim
    active = (lane >= lane_lo) & (lane < lane_lo + span)
    rel = (lane - lane_lo) % dim
    first = rel < half
    gather = rel % half
    cos_t = jnp.where(active[None, :], cos[:, gather], 1.0)
    sin_t = sin[:, gather]
    sin_a = jnp.where((active & first)[None, :], -sin_t, 0.0)
    sin_b = jnp.where((active & ~first)[None, :], sin_t, 0.0)
    return cos_t, sin_a, sin_b


def _lane_pad(x, lo, width=LANES):
    out = jnp.zeros(x.shape[:-1] + (width,), x.dtype)
    return out.at[..., lo:lo + x.shape[-1]].set(x)


def _layer_params(l, a_q_norm, a_k_norm, b_q_a_norm, b_w_uq, b_q_norm, b_kv_a_norm, b_kpe_norm, b_w_uk, b_w_uv,
                  c_mu, c_w0, c_w_up, c_a0, c_a_up, c_k_k, c_k_a, c_r_k, c_gn, d_v_norm, d_ws, d_b,
                  w_br_a, w_br_b, w_br_c, w_br_d, w_out):
    prm = {}
    prm["gq"] = jnp.tile(a_q_norm[l], 4).reshape(1, 256)
    prm["gk"] = jnp.tile(a_k_norm[l], 2).reshape(1, LANES)
    prm["gqa"] = b_q_a_norm[l].reshape(1, Q_LORA)
    prm["wuq"] = _lane_pad(b_w_uq[l], 0).reshape(Q_LORA, H * LANES).astype(BF16)
    prm["gqb"] = _lane_pad(b_q_norm[l], 0).reshape(1, LANES)
    prm["gkv"] = b_kv_a_norm[l].reshape(1, KV_LORA)
    prm["gkpe"] = _lane_pad(b_kpe_norm[l], 64).reshape(1, LANES)
    prm["wuk"] = _lane_pad(b_w_uk[l], 0).reshape(KV_LORA, H * LANES).astype(BF16)
    wukt = jnp.transpose(b_w_uk[l], (1, 2, 0))
    prm["wuk_t"] = jnp.concatenate([wukt, jnp.zeros((H, LANES - NOPE_B, KV_LORA), F32)], axis=1).astype(BF16)
    prm["wuv"] = b_w_uv[l].reshape(KV_LORA, H * HD).astype(BF16)
    wuv_h = jnp.zeros((H, KV_LORA, 256), F32)
    for h in range(H):
        wuv_h = wuv_h.at[h, :, h * HD:(h + 1) * HD].set(b_w_uv[l][:, h, :])
    prm["wuv_heads"] = wuv_h.astype(BF16)
    rw = {}
    rw["mu"] = _lane_pad(c_mu[l], 0, 1024).reshape(1, 1024)
    rw["w0"] = c_w0[l].reshape(1, 256)
    rw["wup"] = jnp.concatenate([c_w_up[l], jnp.zeros((64, 256), F32)], axis=0).astype(BF16)
    rw["a0"] = c_a0[l].reshape(1, 256)
    rw["aup"] = jnp.concatenate([jnp.zeros((64, 256), F32), c_a_up[l]], axis=0).astype(BF16)
    rw["kk"] = c_k_k[l].reshape(1, 256)
    rw["ka"] = c_k_a[l].reshape(1, 256)
    rw["rk"] = c_r_k[l].reshape(1, 256)
    rw["gn"] = c_gn[l].reshape(1, 256)
    prm["rwkv"] = rw
    prm["vn"] = d_v_norm[l].reshape(1, 256)
    prm["ws"] = d_ws[l]
    prm["bias"] = jnp.repeat(d_b[l].T, HD, axis=1)
    prm["ws1"] = jnp.repeat(d_ws[l][:, 0, 0], HD).reshape(1, 256)
    prm["bias1"] = jnp.repeat(d_b[l][:, 0], HD).reshape(1, 256)
    wa = w_br_a[l].reshape(H, HD, D_MODEL)[jnp.array(A_HEAD_ORDER)].reshape(256, D_MODEL)
    prm["wbr"] = jnp.stack([wa, w_br_b[l], w_br_c[l], w_br_d[l]], axis=0).astype(BF16)
    prm["wout"] = w_out[l].astype(BF16)
    return prm


def kernel(x_prompt, x_sample, cache_a_k, cache_a_v, cache_b_latent, cache_b_kpe, state_c_wkv, state_c_shift, page_table, ln_g, w_in, a_q_norm, a_k_norm, b_q_a_norm, b_w_uq, b_q_norm, b_kv_a_norm, b_kpe_norm, b_w_uk, b_w_uv, c_mu, c_w0, c_w_up, c_a0, c_a_up, c_k_k, c_k_a, c_r_k, c_gn, d_v_norm, d_ws, d_b, w_br_a, w_br_b, w_br_c, w_br_d, w_out):
    depth = w_in.shape[0]
    bp, seq, _ = x_prompt.shape
    db, dec_t, _ = x_sample.shape
    n_pool = cache_a_k.shape[1]
    n_pages = page_table.shape[1]
    past = n_pages * PAGE
    assert dec_t == 1 and past % MOBA_BLOCK == 0 and past // MOBA_BLOCK >= MOBA_TOPK
    assert seq % (2 * MOBA_BLOCK) == 0 and seq // MOBA_BLOCK <= LANES
    n = bp * seq
    nblk = seq // MOBA_BLOCK

    ones_bd = jnp.asarray(np.kron(np.eye(4), np.ones((HD, HD))), BF16)
    pos_p = jnp.arange(seq, dtype=jnp.int32)
    pos_s = jnp.full((db,), past, jnp.int32)
    tabs_a_p, tabs_a_s = _rope_tables(pos_p, 0, HD), _rope_tables(pos_s, 0, HD)
    tabs_b_p, tabs_b_s = _rope_tables(pos_p, 64, ROPE_B), _rope_tables(pos_s, 64, ROPE_B)

    ck = jnp.transpose(cache_a_k, (0, 1, 3, 4, 2)).reshape(depth * n_pool, LANES, PAGE)
    cv = jnp.transpose(cache_a_v, (0, 1, 3, 4, 2)).reshape(depth * n_pool, LANES, PAGE)
    cl = cache_b_latent.reshape(depth * n_pool, PAGE, KV_LORA)
    cp = jnp.transpose(cache_b_kpe, (0, 1, 3, 2)).reshape(depth * n_pool, ROPE_B, PAGE)

    tm_prompt = min(1024, n)
    tm_prep = min(2048, seq)
    hp = x_prompt.reshape(n, D_MODEL)
    hs = x_sample.reshape(db, D_MODEL)
    outs = [[] for _ in range(13)]
    for l in range(depth):
        prm = _layer_params(l, a_q_norm, a_k_norm, b_q_a_norm, b_w_uq, b_q_norm, b_kv_a_norm, b_kpe_norm, b_w_uk,
                            b_w_uv, c_mu, c_w0, c_w_up, c_a0, c_a_up, c_k_k, c_k_a, c_r_k, c_gn, d_v_norm, d_ws, d_b,
                            w_br_a, w_br_b, w_br_c, w_br_d, w_out)
        w_packed, w_mg = _pack_w_in(w_in[l])

        pp = _proj(hp, ln_g[l], w_packed, tm_prompt, PROJ_TN)
        qa, ka, kab, vab, kmean = _moba_prep(pp, prm["gq"], prm["gk"], tabs_a_p, ones_bd, tm_prep, seq)
        km_pad = jnp.zeros((bp, LANES, LANES), F32).at[:, :nblk].set(kmean.reshape(bp, nblk, LANES))
        oa = _moba_attn(qa, kab, vab, km_pad, bp, seq)
        qbb, kbb, vbb, lat, kpe = _mla_prep(pp, prm["gqa"], prm["wuq"], prm["gqb"], prm["gkv"], prm["gkpe"],
                                            prm["wuk"], prm["wuv"], tabs_b_p, tm_prep, False)
        ob = _mla_attn(qbb, kbb, vbb, bp, seq, min(512, seq))
        oc, wkv, last = _rwkv_seq(pp, prm["rwkv"], ones_bd, bp, seq, min(256, seq))
        (hp_new,) = _merge(hp, ln_g[l], w_mg, pp, oa, ob, oc, prm["vn"], prm["ws"], prm["bias"], prm["wbr"],
                           prm["wout"], min(256, n), CHUNK_D)
        outs[0].append(ka.reshape(bp, seq, 2, HD))
        outs[1].append(pp[:, P_A + 384:P_A + 512].reshape(bp, seq, 2, HD))
        outs[2].append(lat.reshape(bp, seq, KV_LORA))
        outs[3].append(kpe[:, 64:64 + ROPE_B].reshape(bp, seq, ROPE_B))
        wkv4 = wkv.reshape(bp, H, HD, H, HD)
        outs[4].append(jnp.stack([wkv4[:, h, :, h, :] for h in range(H)], axis=1))
        outs[5].append(last[:, 0, :C_SHIFT])
        hp = hp_new

        ps = _proj(hs, ln_g[l], w_packed, db, PROJ_TN)
        qa, ka, _, _ = _moba_prep(ps, prm["gq"], prm["gk"], tabs_a_s, ones_bd, db, 0)
        va = ps[:, P_A + 384:P_A + 512]
        q4 = qa.reshape(db, 2, 2, HD)
        zeros = jnp.zeros((db, HD), F32)
        q8 = jnp.stack([
            jnp.concatenate([q4[:, 0, 0], zeros], axis=1), jnp.concatenate([q4[:, 1, 0], zeros], axis=1),
            jnp.concatenate([zeros, q4[:, 1, 1]], axis=1), jnp.concatenate([zeros, q4[:, 0, 1]], axis=1),
        ] + [jnp.zeros((db, LANES), F32)] * 4, axis=1)
        idx = _moba_pick(page_table, q8, ck, l * n_pool)
        idx12 = idx[:, :H, :MOBA_TOPK].reshape(db, H * MOBA_TOPK)
        o8 = _moba_decode(idx12, page_table, q8, ka, va, ck, cv, l * n_pool)
        oa_s = jnp.concatenate([o8[:, 0, :HD], o8[:, 3, HD:], o8[:, 1, :HD], o8[:, 2, HD:]], axis=1)

        qlat, qpe, snew, lat_s, kpe_s = _mla_prep(ps, prm["gqa"], prm["wuq"], prm["gqb"], prm["gkv"], prm["gkpe"],
                                                  prm["wuk_t"], prm["wuv"], tabs_b_s, db, True)
        pad8 = lambda x: jnp.concatenate([x, jnp.zeros((db, 8 - H) + x.shape[2:], x.dtype)], axis=1)
        qlat8 = pad8(qlat.reshape(db, H, LANES))
        qpe8 = pad8(qpe.reshape(db, H, LANES)[:, :, 64:64 + ROPE_B])
        snew8 = pad8(snew[:, :H].reshape(db, H, 1))
        olat8 = _mla_decode(page_table, qlat8, qpe8, snew8, lat_s, cl, cp, l * n_pool)
        ob_s = _mla_up(olat8[:, :H].reshape(db, H * LANES), prm["wuv_heads"])

        prev = _lane_pad(state_c_shift[l], 0, 1024)
        oc_s, wkv_s = _rwkv_step(ps, prev, state_c_wkv[l], prm["rwkv"], ones_bd, 8)
        hs_new, v_rows = _merge(hs, ln_g[l], w_mg, ps, oa_s, ob_s, oc_s, prm["vn"], prm["ws1"], prm["bias1"],
                                prm["wbr"], prm["wout"], db, 1)
        outs[6].append(ka.reshape(db, 1, 2, HD))
        outs[7].append(va.reshape(db, 1, 2, HD))
        outs[8].append(lat_s.reshape(db, 1, KV_LORA))
        outs[9].append(kpe_s[:, 64:64 + ROPE_B].reshape(db, 1, ROPE_B))
        outs[10].append(wkv_s)
        outs[11].append(ps[:, P_C:P_C + C_SHIFT])
        outs[12].append(v_rows.reshape(db, 1, 256))
        hs = hs_new

    st = lambda xs: jnp.stack(xs, axis=0)
    return (hp.reshape(bp, seq, D_MODEL), hs.reshape(db, 1, D_MODEL)) + tuple(st(o) for o in outs)
```

```python
import functools

import numpy as np
import jax
import jax.numpy as jnp
from jax import lax
from jax.experimental import pallas as pl
from jax.experimental.pallas import tpu as pltpu

F32 = jnp.float32
BF16 = jnp.bfloat16
HIGHEST = lax.Precision.HIGHEST

D_MODEL = 1024
PAGE = 128
ROPE_THETA = 10000.0
NORM_EPS = 1e-6
GN_EPS = 64e-5
HD = 64
H = 4
W_BR = 256
MOBA_BLOCK = 256
MOBA_TOPK = 3
Q_LORA = 256
KV_LORA = 128
NOPE_B = 64
ROPE_B = 32
C_SHIFT = 896
CHUNK_D = 128
CHUNK_C = 64
NEG = -1e30
LANES = 128

P_GATES, P_A, P_B, P_C, P_D, P_COLS = 0, 1024, 1536, 2048, 3072, 3584
PROJ_TN = 1792
N_MERGE = 4 * D_MODEL
O_AQ, O_AK, O_AV, O_AG = 0, 256, 384, 512
O_BDQ, O_BDKV, O_BG = 768, 1024, 1184
O_CX, O_CG = 1440, 2336
O_DU, O_DV, O_DG, O_MG = 2592, 2848, 3104, 3360
A_HEAD_ORDER = (0, 3, 1, 2)

VMEM_LIMIT = 56 * 1024 * 1024


def _cparams(*sem):
    return pltpu.CompilerParams(dimension_semantics=sem, vmem_limit_bytes=VMEM_LIMIT)


def _dot(a, b, precision=None):
    return jnp.dot(a, b, preferred_element_type=F32, precision=precision)


def _dot_nt(a, b, precision=None):
    return lax.dot_general(a, b, (((1,), (1,)), ((), ())), preferred_element_type=F32, precision=precision)


def _dot_tn(a, b, precision=None):
    return lax.dot_general(a, b, (((0,), (0,)), ((), ())), preferred_element_type=F32, precision=precision)


def _seg_sum(x, ones):
    hi = x.astype(BF16)
    lo = (x - hi.astype(F32)).astype(BF16)
    return _dot(hi, ones) + _dot(lo, ones)


def _rope(x, cos, sin_a, sin_b, half):
    return x * cos + pltpu.roll(x, LANES - half, 1) * sin_a + pltpu.roll(x, half, 1) * sin_b


def _sigmoid(x):
    return 1.0 / (1.0 + jnp.exp(-x))


def _silu(x):
    return x * _sigmoid(x)


def _gelu(x):
    return 0.5 * x * (1.0 + jnp.tanh(np.sqrt(2.0 / np.pi).astype(np.float32) * (x + 0.044715 * (x * x * x))))


def _proj_kernel(x_ref, g_ref, w_ref, o_ref, h_ref):
    @pl.when(pl.program_id(1) == 0)
    def _():
        x = x_ref[...]
        y = x * lax.rsqrt(jnp.mean(x * x, axis=-1, keepdims=True) + NORM_EPS)
        h_ref[...] = (y * g_ref[...]).astype(BF16)

    o_ref[...] = _dot(h_ref[...], w_ref[...])


def _proj(x, g, w, tm, tn):
    n, d = x.shape
    c = w.shape[1]
    return pl.pallas_call(
        _proj_kernel,
        grid=(n // tm, c // tn),
        in_specs=[
            pl.BlockSpec((tm, d), lambda i, j: (i, 0)),
            pl.BlockSpec((1, d), lambda i, j: (0, 0)),
            pl.BlockSpec((d, tn), lambda i, j: (0, j)),
        ],
        out_specs=pl.BlockSpec((tm, tn), lambda i, j: (i, j)),
        out_shape=jax.ShapeDtypeStruct((n, c), F32),
        scratch_shapes=[pltpu.VMEM((tm, d), BF16)],
        compiler_params=_cparams("parallel", "arbitrary"),
        name="proj",
    )(x, g.reshape(1, d), w)


def _moba_prep_kernel(a_ref, gq_ref, gk_ref, cos_ref, sa_ref, sb_ref, ones_ref, q_ref, k_ref, kb_ref, vb_ref,
                      *km_ref, blocks, tiles_per_seq):
    a = a_ref[...]
    ones = ones_ref[...]
    cos, sa, sb = cos_ref[...], sa_ref[...], sb_ref[...]
    q, k, v = a[:, :256], a[:, 256:384], a[:, 384:512]
    qn = q * lax.rsqrt(_seg_sum(q * q, ones) * (1.0 / HD) + NORM_EPS) * gq_ref[...]
    kn = k * lax.rsqrt(_seg_sum(k * k, ones[:LANES, :LANES]) * (1.0 / HD) + NORM_EPS) * gk_ref[...]
    q_ref[:, :LANES] = _rope(qn[:, :LANES], cos, sa, sb, HD // 2)
    q_ref[:, LANES:] = _rope(qn[:, LANES:], cos, sa, sb, HD // 2)
    kr = _rope(kn, cos, sa, sb, HD // 2)
    k_ref[...] = kr
    kb_ref[:, :LANES] = kr.astype(BF16)
    vb_ref[...] = v.astype(BF16)
    if blocks:
        km_ref[0][...] = jnp.mean(kr.reshape(blocks, MOBA_BLOCK, LANES), axis=1)
        tm = a.shape[0]
        first_blk = (pl.program_id(0) % tiles_per_seq) * blocks
        blk = first_blk + (lax.broadcasted_iota(jnp.int32, (tm, LANES), 0) >> 8)
        lane = lax.broadcasted_iota(jnp.int32, (tm, LANES), 1)
        kb_ref[:, LANES:] = jnp.where(lane == blk, 1.0, 0.0).astype(BF16)
    else:
        kb_ref[:, LANES:] = jnp.zeros((a.shape[0], LANES), BF16)


def _moba_prep(p, gq, gk, tabs, ones, tm, seq):
    n = p.shape[0]
    row = lambda i: (i, 0)
    fix = lambda i: (0, 0)
    with_kmean = seq > 0
    assert MOBA_BLOCK == 256 and (not with_kmean or seq % tm == 0)
    blocks = tm // MOBA_BLOCK if with_kmean else 0
    tiles_per_seq = max(seq // tm, 1)
    tab = lambda i: (i % tiles_per_seq, 0)
    out_shape = [
        jax.ShapeDtypeStruct((n, 256), F32),
        jax.ShapeDtypeStruct((n, LANES), F32),
        jax.ShapeDtypeStruct((n, 256), BF16),
        jax.ShapeDtypeStruct((n, LANES), BF16),
    ]
    out_specs = [pl.BlockSpec((tm, 256), row), pl.BlockSpec((tm, LANES), row), pl.BlockSpec((tm, 256), row),
                 pl.BlockSpec((tm, LANES), row)]
    if with_kmean:
        out_shape.append(jax.ShapeDtypeStruct((n // MOBA_BLOCK, LANES), F32))
        out_specs.append(pl.BlockSpec((blocks, LANES), row))
    return pl.pallas_call(
        functools.partial(_moba_prep_kernel, blocks=blocks, tiles_per_seq=tiles_per_seq),
        grid=(n // tm,),
        in_specs=[
            pl.BlockSpec((tm, 512), lambda i: (i, P_A // 512)),
            pl.BlockSpec((1, 256), fix),
            pl.BlockSpec((1, LANES), fix),
            pl.BlockSpec((tm, LANES), tab),
            pl.BlockSpec((tm, LANES), tab),
            pl.BlockSpec((tm, LANES), tab),
            pl.BlockSpec((256, 256), fix),
        ],
        out_specs=out_specs,
        out_shape=out_shape,
        compiler_params=_cparams("parallel"),
        name="moba_prep",
    )(p, gq, gk, *tabs, ones)


def _exp_weights(s, m):
    p = jnp.exp((s - m).astype(BF16))
    t = p
    w = t.shape[1]
    while w > LANES:
        w //= 2
        t = t[:, :w] + t[:, w:]
    return p, jnp.sum(t.astype(F32), axis=-1, keepdims=True)


def _softmax_first(s, v):
    m = jnp.max(s, axis=-1, keepdims=True)
    p, ps = _exp_weights(s, m)
    return m, ps, _dot(p, v)


def _softmax_next(s, v, m, l, acc):
    mn = jnp.maximum(m, jnp.max(s, axis=-1, keepdims=True))
    alpha = jnp.exp(m - mn)
    p, ps = _exp_weights(s, mn)
    return mn, alpha * l + ps, alpha * acc + _dot(p, v)


def _moba_attn_kernel(q_ref, k_ref, v_ref, km_ref, o_ref):
    tq = q_ref.shape[0]
    i = pl.program_id(1)
    q = q_ref[...]
    km = km_ref[0]
    lane = lax.broadcasted_iota(jnp.int32, (tq, LANES), 1)
    lane_f = lane.astype(F32)
    lower = lane < HD

    stacked = []
    for plane in range(2):
        qp = q[:, plane * LANES:(plane + 1) * LANES]
        for half in range(2):
            qm = jnp.where(lower if half == 0 else jnp.logical_not(lower), qp, 0.0)
            gs = jnp.where(lane < i, _dot_nt(qm, km, HIGHEST), NEG)
            bias = jnp.where(lane == i, 0.0, NEG)
            for _ in range(MOBA_TOPK):
                m = jnp.max(gs, axis=1, keepdims=True)
                idx = jnp.min(jnp.where(gs == m, lane_f, float(LANES)), axis=1, keepdims=True)
                pick = jnp.logical_and(lane_f == idx, m > 0.5 * NEG)
                bias = jnp.where(pick, 0.0, bias)
                gs = jnp.where(pick, NEG, gs)
            stacked.append(jnp.concatenate([(qm * (HD ** -0.5)).astype(BF16), bias.astype(BF16)], axis=1))
    qa = jnp.concatenate(stacked, axis=0)

    tk = 2 * tq
    row = lax.broadcasted_iota(jnp.int32, (H * tq, tk), 0) & (tq - 1)
    col = lax.broadcasted_iota(jnp.int32, (H * tq, tk), 1)
    r0 = pl.multiple_of((i // 2) * tk, tk)
    s = jnp.where(col <= row + (i % 2) * tq, _dot_nt(qa, k_ref[pl.ds(r0, tk), :]), NEG)
    init = _softmax_first(s, v_ref[pl.ds(r0, tk), :])

    def body(j, carry):
        c0 = pl.multiple_of(j * tk, tk)
        return _softmax_next(_dot_nt(qa, k_ref[pl.ds(c0, tk), :]), v_ref[pl.ds(c0, tk), :], *carry)

    _, l, acc = lax.fori_loop(0, i // 2, body, init)
    o = acc / l
    o_ref[:, :LANES] = jnp.where(lower, o[0:tq], o[tq:2 * tq])
    o_ref[:, LANES:] = jnp.where(lower, o[2 * tq:3 * tq], o[3 * tq:4 * tq])


def _moba_attn(q, kb, vb, km, batch, seq):
    nq = seq // MOBA_BLOCK
    return pl.pallas_call(
        _moba_attn_kernel,
        grid=(batch, nq),
        in_specs=[
            pl.BlockSpec((MOBA_BLOCK, 256), lambda b, i: (b * nq + i, 0)),
            pl.BlockSpec((seq, 256), lambda b, i: (b, 0)),
            pl.BlockSpec((seq, LANES), lambda b, i: (b, 0)),
            pl.BlockSpec((1, LANES, LANES), lambda b, i: (b, 0, 0)),
        ],
        out_specs=pl.BlockSpec((MOBA_BLOCK, 256), lambda b, i: (b * nq + i, 0)),
        out_shape=jax.ShapeDtypeStruct((batch * seq, 256), F32),
        compiler_params=_cparams("parallel", "arbitrary"),
        name="moba_attn",
    )(q, kb, vb, km)


def _mla_prep_kernel(b_ref, gqa_ref, wuq_ref, gq_ref, gkv_ref, gkpe_ref, wuk_ref, wuv_ref, cos_ref, sa_ref, sb_ref,
                     *out_refs, decode):
    x = b_ref[...]
    cos, sa, sb = cos_ref[...], sa_ref[...], sb_ref[...]
    dq, lat_raw, kpe_raw = x[:, :256], x[:, 256:384], x[:, 384:512]
    cq = dq * lax.rsqrt(jnp.mean(dq * dq, axis=-1, keepdims=True) + NORM_EPS) * gqa_ref[...]
    q = _dot(cq.astype(BF16), wuq_ref[...])
    lat = lat_raw * lax.rsqrt(jnp.mean(lat_raw * lat_raw, axis=-1, keepdims=True) + NORM_EPS) * gkv_ref[...]
    kpe = kpe_raw * lax.rsqrt(jnp.sum(kpe_raw * kpe_raw, axis=-1, keepdims=True) * (1.0 / ROPE_B) + NORM_EPS)
    kpe = _rope(kpe * gkpe_ref[...], cos, sa, sb, ROPE_B // 2)
    scale = (NOPE_B + ROPE_B) ** -0.5
    qs = []
    for h in range(H):
        qh = q[:, h * LANES:(h + 1) * LANES]
        qh = qh * lax.rsqrt(jnp.sum(qh * qh, axis=-1, keepdims=True) * (1.0 / (NOPE_B + ROPE_B)) + NORM_EPS)
        qs.append(_rope(qh * gq_ref[...], cos, sa, sb, ROPE_B // 2) * scale)
    latb = lat.astype(BF16)
    if decode:
        qlat_ref, qpe_ref, snew_ref, lat_ref, kpe_ref = out_refs
        lane = lax.broadcasted_iota(jnp.int32, kpe.shape, 1)
        snew = jnp.zeros(kpe.shape, F32)
        for h in range(H):
            ql = _dot(qs[h].astype(BF16), wuk_ref[h])
            qlat_ref[:, h * LANES:(h + 1) * LANES] = ql
            qpe_ref[:, h * LANES:(h + 1) * LANES] = qs[h]
            s = jnp.sum(ql.astype(BF16).astype(F32) * latb.astype(F32), axis=-1, keepdims=True)
            s = s + jnp.sum(qs[h].astype(BF16).astype(F32) * kpe.astype(BF16).astype(F32), axis=-1, keepdims=True)
            snew = jnp.where(lane == h, s, snew)
        snew_ref[...] = snew
    else:
        qb_ref, kb_ref, vb_ref, lat_ref, kpe_ref = out_refs
        kn = _dot(latb, wuk_ref[...])
        for h in range(H):
            qb_ref[:, h * LANES:(h + 1) * LANES] = qs[h].astype(BF16)
            kb_ref[:, h * LANES:(h + 1) * LANES] = (kn[:, h * LANES:(h + 1) * LANES] + kpe).astype(BF16)
        vb_ref[...] = _dot(latb, wuv_ref[...]).astype(BF16)
    lat_ref[...] = lat
    kpe_ref[...] = kpe


def _mla_prep(p, gqa, wuq, gq, gkv, gkpe, wuk, wuv, tabs, tm, decode):
    n = p.shape[0]
    row = lambda i: (i, 0)
    fix = lambda i: (0, 0)
    tiles_per_tab = tabs[0].shape[0] // tm
    tab = lambda i: (i % tiles_per_tab, 0)
    wide = lambda dt: (jax.ShapeDtypeStruct((n, 512), dt), pl.BlockSpec((tm, 512), row))
    plane = lambda dt: (jax.ShapeDtypeStruct((n, LANES), dt), pl.BlockSpec((tm, LANES), row))
    if decode:
        outs = [wide(F32), wide(F32), plane(F32), plane(F32), plane(F32)]
        wuk_spec = pl.BlockSpec((H, LANES, LANES), lambda i: (0, 0, 0))
    else:
        outs = [wide(BF16), wide(BF16), (jax.ShapeDtypeStruct((n, 256), BF16), pl.BlockSpec((tm, 256), row)),
                plane(F32), plane(F32)]
        wuk_spec = pl.BlockSpec((LANES, 512), fix)
    return pl.pallas_call(
        functools.partial(_mla_prep_kernel, decode=decode),
        grid=(n // tm,),
        in_specs=[
            pl.BlockSpec((tm, 512), lambda i: (i, P_B // 512)),
            pl.BlockSpec((1, 256), fix),
            pl.BlockSpec((256, 512), fix),
            pl.BlockSpec((1, LANES), fix),
            pl.BlockSpec((1, LANES), fix),
            pl.BlockSpec((1, LANES), fix),
            wuk_spec,
            pl.BlockSpec((LANES, 256), fix),
            pl.BlockSpec((tm, LANES), tab),
            pl.BlockSpec((tm, LANES), tab),
            pl.BlockSpec((tm, LANES), tab),
        ],
        out_specs=[s for _, s in outs],
        out_shape=[s for s, _ in outs],
        compiler_params=_cparams("parallel"),
        name="mla_prep",
    )(p, gqa, wuq, gq, gkv, gkpe, wuk, wuv, *tabs)


def _mla_attn_kernel(q_ref, k_ref, v_ref, o_ref):
    tq = q_ref.shape[0]
    i = pl.program_id(1)
    lane = lax.broadcasted_iota(jnp.int32, (tq, LANES), 1)
    lower = lane < HD
    tk = 2 * tq
    row = lax.broadcasted_iota(jnp.int32, (tq, tk), 0)
    col = lax.broadcasted_iota(jnp.int32, (tq, tk), 1)
    r0 = pl.multiple_of((i // 2) * tk, tk)
    qs = [q_ref[:, h * LANES:(h + 1) * LANES] for h in range(H)]

    def tile(c0, h):
        return (k_ref[pl.ds(c0, tk), h * LANES:(h + 1) * LANES],
                v_ref[pl.ds(c0, tk), (h // 2) * LANES:(h // 2 + 1) * LANES])

    causal = col <= row + (i % 2) * tq
    init = []
    for h in range(H):
        kh, vh = tile(r0, h)
        init.extend(_softmax_first(jnp.where(causal, _dot_nt(qs[h], kh), NEG), vh))

    def body(j, carry):
        c0 = pl.multiple_of(j * tk, tk)
        out = []
        for h in range(H):
            kh, vh = tile(c0, h)
            out.extend(_softmax_next(_dot_nt(qs[h], kh), vh, *carry[3 * h:3 * h + 3]))
        return tuple(out)

    fin = lax.fori_loop(0, i // 2, body, tuple(init))
    for plane in range(2):
        o0 = fin[6 * plane + 2] / fin[6 * plane + 1]
        o1 = fin[6 * plane + 5] / fin[6 * plane + 4]
        o_ref[:, plane * LANES:(plane + 1) * LANES] = jnp.where(lower, o0, o1)


def _mla_attn(qb, kb, vb, batch, seq, tq):
    nq = seq // tq
    return pl.pallas_call(
        _mla_attn_kernel,
        grid=(batch, nq),
        in_specs=[
            pl.BlockSpec((tq, 512), lambda b, i: (b * nq + i, 0)),
            pl.BlockSpec((seq, 512), lambda b, i: (b, 0)),
            pl.BlockSpec((seq, 256), lambda b, i: (b, 0)),
        ],
        out_specs=pl.BlockSpec((tq, 256), lambda b, i: (b * nq + i, 0)),
        out_shape=jax.ShapeDtypeStruct((batch * seq, 256), F32),
        compiler_params=_cparams("parallel", "arbitrary"),
        name="mla_attn",
    )(qb, kb, vb)


def _rwkv_features(xc, prev, mu, w0, wup, a0, aup, kk_w, ka_w, ones):
    xs = xc + (prev - xc) * mu
    r, k, v, lora = xs[:, :256], xs[:, 256:512], xs[:, 512:768], xs[:, 768:896]
    w_pre = w0 + _dot(jnp.tanh(lora).astype(BF16), wup)
    z = -w_pre
    w_log = -(jnp.maximum(z, 0.0) + jnp.log(1.0 + jnp.exp(-jnp.abs(z)))) - 0.5
    lw = -jnp.exp(w_log)
    a = _sigmoid(a0 + _dot(lora.astype(BF16), aup))
    kk = k * kk_w
    kk = kk * lax.rsqrt(_seg_sum(kk * kk, ones) + 1e-12)
    k2 = k * (1.0 + (a - 1.0) * ka_w)
    return r, k2, v, kk, kk * a, lw


def _rwkv_out(y, r, k2, v, rk_w, gn_w, ones):
    mean = _seg_sum(y, ones) * (1.0 / HD)
    d = y - mean
    var = _seg_sum(d * d, ones) * (1.0 / HD)
    return d * lax.rsqrt(var + GN_EPS) * gn_w + _seg_sum(r * k2 * rk_w, ones) * v


def _block_stack(x, same_blk):
    return jnp.where(same_blk, jnp.concatenate([x, x, x, x], axis=0), 0.0)


def _row_blocks_sum(x):
    c = CHUNK_C
    return x[0:c] + x[c:2 * c] + x[2 * c:3 * c] + x[3 * c:4 * c]


def _rwkv_seq_kernel(x_ref, mu_ref, w0_ref, wup_ref, a0_ref, aup_ref, kk_ref, ka_ref, rk_ref, gn_ref, ones_ref,
                     o_ref, s_out_ref, last_ref, s_ref, prev_ref):
    tt = x_ref.shape[0]
    c = CHUNK_C

    @pl.when(pl.program_id(1) == 0)
    def _():
        s_ref[...] = jnp.zeros(s_ref.shape, F32)
        prev_ref[...] = jnp.zeros(prev_ref.shape, F32)

    ones = ones_ref[...]
    xc = x_ref[...]
    rid = lax.broadcasted_iota(jnp.int32, xc.shape, 0)
    prev = jnp.where(rid == 0, prev_ref[...], pltpu.roll(xc, 1, 0))
    prev_ref[...] = xc[tt - 1:tt, :]
    last_ref[0] = xc[tt - 1:tt, :]
    r, k2, v, kk, bv, lw = _rwkv_features(xc, prev, mu_ref[...], w0_ref[...], wup_ref[...], a0_ref[...], aup_ref[...],
                                          kk_ref[...], ka_ref[...], ones)

    rr = lax.broadcasted_iota(jnp.int32, (256, 256), 0)
    cc = lax.broadcasted_iota(jnp.int32, (256, 256), 1)
    same_blk = (rr >> 6) == (cc >> 6)
    strict = (rr & 63) > (cc & 63)
    incl = (rr & 63) >= (cc & 63)
    eye = jnp.where(rr == cc, 1.0, 0.0)
    tri = jnp.where(incl[:c, :c], 1.0, 0.0)

    chunks = []
    for ci in range(tt // c):
        sl = slice(ci * c, (ci + 1) * c)
        lw_c, r_c, k_c, v_c, kk_c, bv_c = lw[sl], r[sl], k2[sl], v[sl], kk[sl], bv[sl]
        cum = _dot(tri, lw_c, HIGHEST)
        last = cum[c - 1:c, :]
        e_neg = jnp.exp(-cum)
        e_end = jnp.exp(last - cum)
        chunks.append(dict(
            a=_block_stack(-kk_c * jnp.exp(cum - lw_c), same_blk).astype(BF16),
            r=_block_stack(r_c * jnp.exp(cum), same_blk).astype(BF16),
            b=_block_stack(bv_c * e_neg, same_blk).astype(BF16),
            k=_block_stack(k_c * e_neg, same_blk).astype(BF16),
            v=_block_stack(v_c, same_blk).astype(BF16),
            b_end=_block_stack(bv_c * e_end, same_blk).astype(BF16),
            k_end=_block_stack(k_c * e_end, same_blk).astype(BF16), decay=jnp.exp(last)))
    for ch in chunks:
        ch["n_ab"] = jnp.where(strict, _dot_nt(ch["a"], ch["b"]), 0.0)
        ch["n_ak"] = jnp.where(strict, _dot_nt(ch["a"], ch["k"]), 0.0).astype(BF16)
        ch["n_rb"] = jnp.where(incl, _dot_nt(ch["r"], ch["b"]), 0.0).astype(BF16)
        ch["n_rk"] = jnp.where(incl, _dot_nt(ch["r"], ch["k"]), 0.0).astype(BF16)
        ch["t_inv"] = eye + ch["n_ab"]
        ch["pw"] = ch["n_ab"].astype(BF16)
    for _ in range(5):
        for ch in chunks:
            ch["pw"] = _dot(ch["pw"], ch["pw"]).astype(BF16)
        for ch in chunks:
            ch["t_inv"] = ch["t_inv"] + _dot(ch["t_inv"].astype(BF16), ch["pw"])
    for ch in chunks:
        t_inv = ch["t_inv"].astype(BF16)
        akv = _dot(ch["n_ak"], ch["v"]).astype(BF16)
        ch["w"] = _dot(t_inv, ch["a"]).astype(BF16)
        ch["u0"] = _dot(t_inv, akv).astype(BF16)
    for ch in chunks:
        ch["r2"] = (ch["r"].astype(F32) + _dot(ch["n_rb"], ch["w"])).astype(BF16)
        ch["y0"] = _dot(ch["n_rb"], ch["u0"]) + _dot(ch["n_rk"], ch["v"])
        ch["m"] = _dot_tn(ch["w"], ch["b_end"]).astype(BF16)
        ch["g"] = _dot_tn(ch["u0"], ch["b_end"]) + _dot_tn(ch["v"], ch["k_end"])

    s = s_ref[...]
    ys = []
    for ch in chunks:
        sb = s.astype(BF16)
        ys.append(_row_blocks_sum(_dot_nt(ch["r2"], sb) + ch["y0"]))
        s = s * ch["decay"] + _dot(sb, ch["m"]) + ch["g"]
    s_ref[...] = s
    s_out_ref[0] = s
    y = jnp.concatenate(ys, axis=0)
    o_ref[...] = _rwkv_out(y, r, k2, v, rk_ref[...], gn_ref[...], ones)


def _rwkv_seq(p, prm, ones, batch, seq, tt):
    nt = seq // tt
    fix = lambda b, t: (0, 0)
    vec256 = pl.BlockSpec((1, 256), fix)
    return pl.pallas_call(
        _rwkv_seq_kernel,
        grid=(batch, nt),
        in_specs=[
            pl.BlockSpec((tt, 1024), lambda b, t: (b * nt + t, P_C // 1024)),
            pl.BlockSpec((1, 1024), fix),
            vec256, pl.BlockSpec((LANES, 256), fix), vec256, pl.BlockSpec((LANES, 256), fix),
            vec256, vec256, vec256, vec256,
            pl.BlockSpec((256, 256), fix),
        ],
        out_specs=[
            pl.BlockSpec((tt, 256), lambda b, t: (b * nt + t, 0)),
            pl.BlockSpec((1, 256, 256), lambda b, t: (b, 0, 0)),
            pl.BlockSpec((1, 1, 1024), lambda b, t: (b, 0, 0)),
        ],
        out_shape=[
            jax.ShapeDtypeStruct((batch * seq, 256), F32),
            jax.ShapeDtypeStruct((batch, 256, 256), F32),
            jax.ShapeDtypeStruct((batch, 1, 1024), F32),
        ],
        scratch_shapes=[pltpu.VMEM((256, 256), F32), pltpu.VMEM((1, 1024), F32)],
        compiler_params=_cparams("parallel", "arbitrary"),
        name="rwkv_seq",
    )(p, prm["mu"], prm["w0"], prm["wup"], prm["a0"], prm["aup"], prm["kk"], prm["ka"], prm["rk"], prm["gn"], ones)


def _rwkv_step_kernel(x_ref, prev_ref, s_ref, mu_ref, w0_ref, wup_ref, a0_ref, aup_ref, kk_ref, ka_ref, rk_ref,
                      gn_ref, ones_ref, o_ref, s_out_ref, yt_ref):
    nb = x_ref.shape[0]
    ones = ones_ref[...]
    r, k2, v, kk, bv, lw = _rwkv_features(x_ref[...], prev_ref[...], mu_ref[...], w0_ref[...], wup_ref[...],
                                          a0_ref[...], aup_ref[...], kk_ref[...], ka_ref[...], ones)
    w = jnp.exp(lw)
    vt = v.T
    for b in range(nb):
        for h in range(H):
            hs = slice(h * HD, (h + 1) * HD)
            st = s_ref[b, h]
            sa = -jnp.sum(st * kk[b:b + 1, hs], axis=-1, keepdims=True)
            st = st * w[b:b + 1, hs] + sa * bv[b:b + 1, hs] + vt[hs, b:b + 1] * k2[b:b + 1, hs]
            s_out_ref[b, h] = st
            yt_ref[hs, b:b + 1] = jnp.sum(st * r[b:b + 1, hs], axis=-1, keepdims=True)
    y = yt_ref[...].T
    o_ref[...] = _rwkv_out(y, r, k2, v, rk_ref[...], gn_ref[...], ones)


def _rwkv_step(p, prev, state, prm, ones, nb):
    n = p.shape[0]
    fix = lambda i: (0, 0)
    vec256 = pl.BlockSpec((1, 256), fix)
    return pl.pallas_call(
        _rwkv_step_kernel,
        grid=(n // nb,),
        in_specs=[
            pl.BlockSpec((nb, 1024), lambda i: (i, P_C // 1024)),
            pl.BlockSpec((nb, 1024), lambda i: (i, 0)),
            pl.BlockSpec((nb, H, HD, HD), lambda i: (i, 0, 0, 0)),
            pl.BlockSpec((1, 1024), fix),
            vec256, pl.BlockSpec((LANES, 256), fix), vec256, pl.BlockSpec((LANES, 256), fix),
            vec256, vec256, vec256, vec256,
            pl.BlockSpec((256, 256), fix),
        ],
        out_specs=[
            pl.BlockSpec((nb, 256), lambda i: (i, 0)),
            pl.BlockSpec((nb, H, HD, HD), lambda i: (i, 0, 0, 0)),
        ],
        out_shape=[
            jax.ShapeDtypeStruct((n, 256), F32),
            jax.ShapeDtypeStruct((n, H, HD, HD), F32),
        ],
        scratch_shapes=[pltpu.VMEM((256, nb), F32)],
        compiler_params=_cparams("parallel"),
        name="rwkv_step",
    )(p, prev, state, prm["mu"], prm["w0"], prm["wup"], prm["a0"], prm["aup"], prm["kk"], prm["ka"], prm["rk"],
      prm["gn"], ones)


def _merge_kernel(x_ref, lng_ref, wmg_ref, g_ref, oa_ref, ob_ref, oc_ref, d_ref, vn_ref, ws_ref, bias_ref, wbr_ref,
                  wout_ref, o_ref, *v_ref, chunk):
    tm = x_ref.shape[0]
    x = x_ref[...]
    hb = (x * lax.rsqrt(jnp.mean(x * x, axis=-1, keepdims=True) + NORM_EPS) * lng_ref[...]).astype(BF16)
    d = d_ref[...]
    u = _gelu(d[:, :256])
    vg = _gelu(d[:, 256:512])
    v = vg * lax.rsqrt(jnp.mean(vg * vg, axis=-1, keepdims=True) + NORM_EPS) * vn_ref[...]
    if chunk == 1:
        sp = v * ws_ref[...] + bias_ref[...]
        v_ref[0][...] = v
    else:
        rr = lax.broadcasted_iota(jnp.int32, (chunk, chunk), 0)
        cc = lax.broadcasted_iota(jnp.int32, (chunk, chunk), 1)
        grp = lax.broadcasted_iota(jnp.int32, (chunk, 256), 1) >> 6
        vb = v.astype(BF16)
        ws = [jnp.where(cc <= rr, ws_ref[g], 0.0).astype(BF16) for g in range(H)]
        parts = []
        for ci in range(tm // chunk):
            vc = vb[ci * chunk:(ci + 1) * chunk]
            acc = bias_ref[...]
            for g in range(H):
                acc = acc + jnp.where(grp == g, _dot(ws[g], vc), 0.0)
            parts.append(acc)
        sp = jnp.concatenate(parts, axis=0)
    od = u * sp
    gates = g_ref[...]
    y = jnp.zeros((tm, D_MODEL), F32)
    for bi, o in enumerate((oa_ref[...], ob_ref[...], oc_ref[...], od)):
        z = (o * _silu(gates[:, bi * 256:(bi + 1) * 256])).astype(BF16)
        mg = _dot(hb, wmg_ref[:, bi * D_MODEL:(bi + 1) * D_MODEL])
        y = y + _sigmoid(mg) * _dot(z, wbr_ref[bi])
    o_ref[...] = x + _dot(y.astype(BF16), wout_ref[...])


def _merge(x, lng, wmg, p, oa, ob, oc, vn, ws, bias, wbr, wout, tm, chunk):
    n = x.shape[0]
    row = lambda i: (i, 0)
    fix = lambda i: (0, 0)
    out_shape = [jax.ShapeDtypeStruct((n, D_MODEL), F32)]
    out_specs = [pl.BlockSpec((tm, D_MODEL), row)]
    if chunk == 1:
        out_shape.append(jax.ShapeDtypeStruct((n, 256), F32))
        out_specs.append(pl.BlockSpec((tm, 256), row))
        ws_spec = pl.BlockSpec((1, 256), fix)
        bias_spec = pl.BlockSpec((1, 256), fix)
    else:
        ws_spec = pl.BlockSpec((H, chunk, chunk), lambda i: (0, 0, 0))
        bias_spec = pl.BlockSpec((chunk, 256), fix)
    return pl.pallas_call(
        functools.partial(_merge_kernel, chunk=chunk),
        grid=(n // tm,),
        in_specs=[
            pl.BlockSpec((tm, D_MODEL), row),
            pl.BlockSpec((1, D_MODEL), fix),
            pl.BlockSpec((D_MODEL, N_MERGE), fix),
            pl.BlockSpec((tm, 1024), lambda i: (i, P_GATES // 1024)),
            pl.BlockSpec((tm, 256), row),
            pl.BlockSpec((tm, 256), row),
            pl.BlockSpec((tm, 256), row),
            pl.BlockSpec((tm, 512), lambda i: (i, P_D // 512)),
            pl.BlockSpec((1, 256), fix),
            ws_spec,
            bias_spec,
            pl.BlockSpec((H, 256, D_MODEL), lambda i: (0, 0, 0)),
            pl.BlockSpec((D_MODEL, D_MODEL), fix),
        ],
        out_specs=out_specs,
        out_shape=out_shape,
        compiler_params=_cparams("parallel"),
        name="merge",
    )(x, lng.reshape(1, D_MODEL), wmg, p, oa, ob, oc, p, vn, ws, bias, wbr, wout)


def _page_pipeline(b, n_rows, copies_of, n_copies):
    slot = b % 2

    def start_row(row, sl):
        for i in range(n_copies):
            for j, cp in enumerate(copies_of(row, sl, i)):
                cp.start(priority=(i + j) % 2)

    @pl.when(b == 0)
    def _():
        start_row(0, 0)

    for i in range(n_copies):
        for cp in copies_of(b, slot, i):
            cp.wait()
    start_row(jnp.minimum(b + 1, n_rows - 1), 1 - slot)
    return slot


def _page_drain(b, n_rows, copies_of, n_copies):
    @pl.when(b == n_rows - 1)
    def _():
        for i in range(n_copies):
            for cp in copies_of(b, 1 - b % 2, i):
                cp.wait()


def _moba_pick_kernel(pt_ref, q_ref, k_hbm, idx_ref, k_buf, sem, *, base, n_pages):
    b = pl.program_id(0)

    def copies_of(row, sl, i):
        return (pltpu.make_async_copy(k_hbm.at[base + pt_ref[row, i]], k_buf.at[sl, :, pl.ds(i * PAGE, PAGE)],
                                      sem.at[sl]),)

    slot = _page_pipeline(b, pl.num_programs(0), copies_of, n_pages)
    nblk = n_pages * PAGE // MOBA_BLOCK
    q = q_ref[0]
    hi = q.astype(BF16).astype(F32)
    q2 = jnp.concatenate([hi, q - hi], axis=0).astype(BF16)
    s2 = _dot(q2, k_buf[slot].astype(BF16))
    s = s2[:8] + s2[8:]
    lane = lax.broadcasted_iota(jnp.int32, (8, LANES), 1)
    lane_f = lane.astype(F32)
    gs = jnp.full((8, LANES), NEG, F32)
    for blk in range(nblk):
        tot = jnp.sum(s[:, blk * MOBA_BLOCK:(blk + 1) * MOBA_BLOCK], axis=-1, keepdims=True)
        gs = jnp.where(lane == blk, tot * (1.0 / MOBA_BLOCK), gs)
    out = jnp.zeros((8, LANES), jnp.int32)
    for t in range(MOBA_TOPK):
        m = jnp.max(gs, axis=1, keepdims=True)
        idx = jnp.min(jnp.where(gs == m, lane_f, float(LANES)), axis=1, keepdims=True)
        out = jnp.where(lane == t, idx.astype(jnp.int32), out)
        gs = jnp.where(lane_f == idx, NEG, gs)
    idx_ref[0] = out
    _page_drain(b, pl.num_programs(0), copies_of, n_pages)


def _moba_pick(page_table, q8, cache_kt, layer_base):
    nb, n_pages = page_table.shape
    assert n_pages * PAGE // MOBA_BLOCK <= LANES
    grid_spec = pltpu.PrefetchScalarGridSpec(
        num_scalar_prefetch=1,
        grid=(nb,),
        in_specs=[pl.BlockSpec((1, 8, LANES), lambda b, pt: (b, 0, 0)), pl.BlockSpec(memory_space=pl.ANY)],
        out_specs=pl.BlockSpec((1, 8, LANES), lambda b, pt: (b, 0, 0)),
        scratch_shapes=[pltpu.VMEM((2, LANES, n_pages * PAGE), F32), pltpu.SemaphoreType.DMA((2,))],
    )
    return pl.pallas_call(
        functools.partial(_moba_pick_kernel, base=layer_base, n_pages=n_pages),
        grid_spec=grid_spec,
        out_shape=jax.ShapeDtypeStruct((nb, 8, LANES), jnp.int32),
        compiler_params=_cparams("arbitrary"),
        name="moba_pick",
    )(page_table, q8, cache_kt)


def _moba_decode_kernel(idx_ref, pt_ref, q_ref, kn_ref, vn_ref, k_hbm, v_hbm, o_ref, k_buf, v_buf, sem, *, base):
    b = pl.program_id(0)
    per_blk = MOBA_BLOCK // PAGE
    per_head = MOBA_TOPK * per_blk

    def copies_of(row, sl, i):
        out = []
        for h in range(H):
            for t in range(MOBA_TOPK):
                for pi in range(per_blk):
                    page = base + pt_ref[row, idx_ref[row, h * MOBA_TOPK + t] * per_blk + pi]
                    dst = pl.ds((t * per_blk + pi) * PAGE, PAGE)
                    out.append(pltpu.make_async_copy(k_hbm.at[page], k_buf.at[sl, h, :, dst], sem.at[0, sl]))
                    out.append(pltpu.make_async_copy(v_hbm.at[page], v_buf.at[sl, h, :, dst], sem.at[1, sl]))
        return out

    slot = _page_pipeline(b, pl.num_programs(0), copies_of, 1)
    q = q_ref[0]
    qb = (q * (HD ** -0.5)).astype(BF16)
    kn = kn_ref[0].astype(BF16).astype(F32)
    vn = vn_ref[0].astype(BF16).astype(F32)
    s_own = jnp.sum(qb.astype(F32) * kn, axis=-1, keepdims=True)
    rowid = lax.broadcasted_iota(jnp.int32, (8, LANES), 0)
    out = jnp.zeros((8, LANES), F32)
    for h in range(H):
        s = _dot(qb, k_buf[slot, h].astype(BF16))
        m = jnp.maximum(s_own, jnp.max(s, axis=-1, keepdims=True))
        p_own = jnp.exp(s_own - m)
        p = jnp.exp(s - m)
        l = p_own + jnp.sum(p, axis=-1, keepdims=True)
        acc = p_own.astype(BF16).astype(F32) * vn + _dot_nt(p.astype(BF16), v_buf[slot, h].astype(BF16))
        out = jnp.where(rowid == h, acc / l, out)
    o_ref[0] = out
    _page_drain(b, pl.num_programs(0), copies_of, 1)


def _moba_decode(idx, page_table, q8, k_new, v_new, cache_kt, cache_vt, layer_base):
    nb = page_table.shape[0]
    sel_tokens = MOBA_TOPK * MOBA_BLOCK
    vec = pl.BlockSpec((1, 1, LANES), lambda b, idx_r, pt: (b, 0, 0))
    grid_spec = pltpu.PrefetchScalarGridSpec(
        num_scalar_prefetch=2,
        grid=(nb,),
        in_specs=[pl.BlockSpec((1, 8, LANES), lambda b, idx_r, pt: (b, 0, 0)), vec, vec,
                  pl.BlockSpec(memory_space=pl.ANY), pl.BlockSpec(memory_space=pl.ANY)],
        out_specs=pl.BlockSpec((1, 8, LANES), lambda b, idx_r, pt: (b, 0, 0)),
        scratch_shapes=[pltpu.VMEM((2, H, LANES, sel_tokens), F32), pltpu.VMEM((2, H, LANES, sel_tokens), F32),
                        pltpu.SemaphoreType.DMA((2, 2))],
    )
    return pl.pallas_call(
        functools.partial(_moba_decode_kernel, base=layer_base),
        grid_spec=grid_spec,
        out_shape=jax.ShapeDtypeStruct((nb, 8, LANES), F32),
        compiler_params=_cparams("arbitrary"),
        name="moba_decode",
    )(idx, page_table, q8, k_new.reshape(nb, 1, LANES), v_new.reshape(nb, 1, LANES), cache_kt, cache_vt)


def _mla_decode_kernel(pt_ref, ql_ref, qp_ref, sn_ref, ln_ref, lat_hbm, kpe_hbm, o_ref, lat_buf, kpe_buf, sem, *,
                       base, n_pages):
    b = pl.program_id(0)

    def copies_of(row, sl, i):
        page = base + pt_ref[row, i]
        off = pl.ds(i * PAGE, PAGE)
        return (pltpu.make_async_copy(lat_hbm.at[page], lat_buf.at[sl, off, :], sem.at[0, sl]),
                pltpu.make_async_copy(kpe_hbm.at[page], kpe_buf.at[sl, :, off], sem.at[1, sl]))

    slot = _page_pipeline(b, pl.num_programs(0), copies_of, n_pages)
    ql = ql_ref[0].astype(BF16)
    qp = qp_ref[0].astype(BF16)
    lat = lat_buf[slot].astype(BF16)
    s = _dot_nt(ql, lat) + _dot(qp, kpe_buf[slot].astype(BF16))
    s_new = sn_ref[0]
    m = jnp.maximum(s_new, jnp.max(s, axis=-1, keepdims=True))
    p_new = jnp.exp(s_new - m)
    p = jnp.exp(s - m).astype(BF16)
    l = p_new + jnp.sum(p.astype(F32), axis=-1, keepdims=True)
    acc = p_new.astype(BF16).astype(F32) * ln_ref[0].astype(BF16).astype(F32) + _dot(p, lat)
    o_ref[0] = acc / l
    _page_drain(b, pl.num_programs(0), copies_of, n_pages)


def _mla_decode(page_table, qlat8, qpe8, snew8, lat_new, cache_lat, cache_kpet, layer_base):
    nb, n_pages = page_table.shape
    per_b = lambda shape: pl.BlockSpec((1,) + shape, lambda b, pt: (b, 0, 0))
    grid_spec = pltpu.PrefetchScalarGridSpec(
        num_scalar_prefetch=1,
        grid=(nb,),
        in_specs=[per_b((8, LANES)), per_b((8, ROPE_B)), per_b((8, 1)), per_b((1, LANES)),
                  pl.BlockSpec(memory_space=pl.ANY), pl.BlockSpec(memory_space=pl.ANY)],
        out_specs=per_b((8, LANES)),
        scratch_shapes=[pltpu.VMEM((2, n_pages * PAGE, KV_LORA), F32), pltpu.VMEM((2, ROPE_B, n_pages * PAGE), F32),
                        pltpu.SemaphoreType.DMA((2, 2))],
    )
    return pl.pallas_call(
        functools.partial(_mla_decode_kernel, base=layer_base, n_pages=n_pages),
        grid_spec=grid_spec,
        out_shape=jax.ShapeDtypeStruct((nb, 8, LANES), F32),
        compiler_params=_cparams("arbitrary"),
        name="mla_decode",
    )(page_table, qlat8, qpe8, snew8, lat_new.reshape(nb, 1, LANES), cache_lat, cache_kpet)


def _mla_up_kernel(o_ref, w_ref, out_ref):
    acc = jnp.zeros(out_ref.shape, F32)
    for h in range(H):
        acc = acc + _dot(o_ref[:, h * LANES:(h + 1) * LANES].astype(BF16), w_ref[h])
    out_ref[...] = acc


def _mla_up(o_lat, wuv_heads):
    n = o_lat.shape[0]
    return pl.pallas_call(
        _mla_up_kernel,
        out_shape=jax.ShapeDtypeStruct((n, 256), F32),
        name="mla_up",
    )(o_lat, wuv_heads)


def _pack_w_in(w):
    d = w.shape[0]
    z = lambda n: jnp.zeros((d, n), w.dtype)
    col = lambda o, n: w[:, o:o + n]
    aq = jnp.concatenate([col(O_AQ + h * HD, HD) for h in A_HEAD_ORDER], axis=1)
    ag = jnp.concatenate([col(O_AG + h * HD, HD) for h in A_HEAD_ORDER], axis=1)
    parts = [
        ag, col(O_BG, 256), col(O_CG, 256), col(O_DG, 256),
        aq, col(O_AK, 128), col(O_AV, 128),
        col(O_BDQ, 256), col(O_BDKV, KV_LORA), z(64), col(O_BDKV + KV_LORA, ROPE_B), z(32),
        col(O_CX, C_SHIFT), z(128),
        col(O_DU, 256), col(O_DV, 256),
    ]
    out = jnp.concatenate(parts, axis=1).astype(BF16)
    assert out.shape[1] == P_COLS
    return out, col(O_MG, N_MERGE).astype(BF16)


def _rope_tables(pos, lane_lo, dim):
    half = dim // 2
    inv = ROPE_THETA ** (-jnp.arange(half, dtype=F32) / half)
    ang = pos.astype(F32)[:, None] * inv[None, :]
    cos, sin = jnp.cos(ang), jnp.sin(ang)
    lane = np.arange(LANES)
    span = LANES if dim == HD else dim
    active = (lane >= lane_lo) & (lane < lane_lo + span)
    rel = (lane - lane_lo) % dim
    first = rel < half
    gather = rel % half
    cos_t = jnp.where(active[None, :], cos[:, gather], 1.0)
    sin_t = sin[:, gather]
    sin_a = jnp.where((active & first)[None, :], -sin_t, 0.0)
    sin_b = jnp.where((active & ~first)[None, :], sin_t, 0.0)
    return cos_t, sin_a, sin_b


def _lane_pad(x, lo, width=LANES):
    out = jnp.zeros(x.shape[:-1] + (width,), x.dtype)
    return out.at[..., lo:lo + x.shape[-1]].set(x)


def _layer_params(l, a_q_norm, a_k_norm, b_q_a_norm, b_w_uq, b_q_norm, b_kv_a_norm, b_kpe_norm, b_w_uk, b_w_uv,
                  c_mu, c_w0, c_w_up, c_a0, c_a_up, c_k_k, c_k_a, c_r_k, c_gn, d_v_norm, d_ws, d_b,
                  w_br_a, w_br_b, w_br_c, w_br_d, w_out):
    prm = {}
    prm["gq"] = jnp.tile(a_q_norm[l], 4).reshape(1, 256)
    prm["gk"] = jnp.tile(a_k_norm[l], 2).reshape(1, LANES)
    prm["gqa"] = b_q_a_norm[l].reshape(1, Q_LORA)
    prm["wuq"] = _lane_pad(b_w_uq[l], 0).reshape(Q_LORA, H * LANES).astype(BF16)
    prm["gqb"] = _lane_pad(b_q_norm[l], 0).reshape(1, LANES)
    prm["gkv"] = b_kv_a_norm[l].reshape(1, KV_LORA)
    prm["gkpe"] = _lane_pad(b_kpe_norm[l], 64).reshape(1, LANES)
    prm["wuk"] = _lane_pad(b_w_uk[l], 0).reshape(KV_LORA, H * LANES).astype(BF16)
    wukt = jnp.transpose(b_w_uk[l], (1, 2, 0))
    prm["wuk_t"] = jnp.concatenate([wukt, jnp.zeros((H, LANES - NOPE_B, KV_LORA), F32)], axis=1).astype(BF16)
    prm["wuv"] = b_w_uv[l].reshape(KV_LORA, H * HD).astype(BF16)
    wuv_h = jnp.zeros((H, KV_LORA, 256), F32)
    for h in range(H):
        wuv_h = wuv_h.at[h, :, h * HD:(h + 1) * HD].set(b_w_uv[l][:, h, :])
    prm["wuv_heads"] = wuv_h.astype(BF16)
    rw = {}
    rw["mu"] = _lane_pad(c_mu[l], 0, 1024).reshape(1, 1024)
    rw["w0"] = c_w0[l].reshape(1, 256)
    rw["wup"] = jnp.concatenate([c_w_up[l], jnp.zeros((64, 256), F32)], axis=0).astype(BF16)
    rw["a0"] = c_a0[l].reshape(1, 256)
    rw["aup"] = jnp.concatenate([jnp.zeros((64, 256), F32), c_a_up[l]], axis=0).astype(BF16)
    rw["kk"] = c_k_k[l].reshape(1, 256)
    rw["ka"] = c_k_a[l].reshape(1, 256)
    rw["rk"] = c_r_k[l].reshape(1, 256)
    rw["gn"] = c_gn[l].reshape(1, 256)
    prm["rwkv"] = rw
    prm["vn"] = d_v_norm[l].reshape(1, 256)
    prm["ws"] = d_ws[l]
    prm["bias"] = jnp.repeat(d_b[l].T, HD, axis=1)
    prm["ws1"] = jnp.repeat(d_ws[l][:, 0, 0], HD).reshape(1, 256)
    prm["bias1"] = jnp.repeat(d_b[l][:, 0], HD).reshape(1, 256)
    wa = w_br_a[l].reshape(H, HD, D_MODEL)[jnp.array(A_HEAD_ORDER)].reshape(256, D_MODEL)
    prm["wbr"] = jnp.stack([wa, w_br_b[l], w_br_c[l], w_br_d[l]], axis=0).astype(BF16)
    prm["wout"] = w_out[l].astype(BF16)
    return prm


def kernel(x_prompt, x_sample, cache_a_k, cache_a_v, cache_b_latent, cache_b_kpe, state_c_wkv, state_c_shift, page_table, ln_g, w_in, a_q_norm, a_k_norm, b_q_a_norm, b_w_uq, b_q_norm, b_kv_a_norm, b_kpe_norm, b_w_uk, b_w_uv, c_mu, c_w0, c_w_up, c_a0, c_a_up, c_k_k, c_k_a, c_r_k, c_gn, d_v_norm, d_ws, d_b, w_br_a, w_br_b, w_br_c, w_br_d, w_out):
    depth = w_in.shape[0]
    bp, seq, _ = x_prompt.shape
    db, dec_t, _ = x_sample.shape
    n_pool = cache_a_k.shape[1]
    n_pages = page_table.shape[1]
    past = n_pages * PAGE
    assert dec_t == 1 and past % MOBA_BLOCK == 0 and past // MOBA_BLOCK >= MOBA_TOPK
    assert seq % (2 * MOBA_BLOCK) == 0 and seq // MOBA_BLOCK <= LANES
    n = bp * seq
    nblk = seq // MOBA_BLOCK

    ones_bd = jnp.asarray(np.kron(np.eye(4), np.ones((HD, HD))), BF16)
    pos_p = jnp.arange(seq, dtype=jnp.int32)
    pos_s = jnp.full((db,), past, jnp.int32)
    tabs_a_p, tabs_a_s = _rope_tables(pos_p, 0, HD), _rope_tables(pos_s, 0, HD)
    tabs_b_p, tabs_b_s = _rope_tables(pos_p, 64, ROPE_B), _rope_tables(pos_s, 64, ROPE_B)

    ck = jnp.transpose(cache_a_k, (0, 1, 3, 4, 2)).reshape(depth * n_pool, LANES, PAGE)
    cv = jnp.transpose(cache_a_v, (0, 1, 3, 4, 2)).reshape(depth * n_pool, LANES, PAGE)
    cl = cache_b_latent.reshape(depth * n_pool, PAGE, KV_LORA)
    cp = jnp.transpose(cache_b_kpe, (0, 1, 3, 2)).reshape(depth * n_pool, ROPE_B, PAGE)

    tm_prompt = min(1024, n)
    tm_prep = min(2048, seq)
    hp = x_prompt.reshape(n, D_MODEL)
    hs = x_sample.reshape(db, D_MODEL)
    outs = [[] for _ in range(13)]
    for l in range(depth):
        prm = _layer_params(l, a_q_norm, a_k_norm, b_q_a_norm, b_w_uq, b_q_norm, b_kv_a_norm, b_kpe_norm, b_w_uk,
                            b_w_uv, c_mu, c_w0, c_w_up, c_a0, c_a_up, c_k_k, c_k_a, c_r_k, c_gn, d_v_norm, d_ws, d_b,
                            w_br_a, w_br_b, w_br_c, w_br_d, w_out)
        w_packed, w_mg = _pack_w_in(w_in[l])

        pp = _proj(hp, ln_g[l], w_packed, tm_prompt, PROJ_TN)
        qa, ka, kab, vab, kmean = _moba_prep(pp, prm["gq"], prm["gk"], tabs_a_p, ones_bd, tm_prep, seq)
        km_pad = jnp.zeros((bp, LANES, LANES), F32).at[:, :nblk].set(kmean.reshape(bp, nblk, LANES))
        oa = _moba_attn(qa, kab, vab, km_pad, bp, seq)
        qbb, kbb, vbb, lat, kpe = _mla_prep(pp, prm["gqa"], prm["wuq"], prm["gqb"], prm["gkv"], prm["gkpe"],
                                            prm["wuk"], prm["wuv"], tabs_b_p, tm_prep, False)
        ob = _mla_attn(qbb, kbb, vbb, bp, seq, min(512, seq))
        oc, wkv, last = _rwkv_seq(pp, prm["rwkv"], ones_bd, bp, seq, min(512, seq))
        (hp_new,) = _merge(hp, ln_g[l], w_mg, pp, oa, ob, oc, prm["vn"], prm["ws"], prm["bias"], prm["wbr"],
                           prm["wout"], min(256, n), CHUNK_D)
        outs[0].append(ka.reshape(bp, seq, 2, HD))
        outs[1].append(pp[:, P_A + 384:P_A + 512].reshape(bp, seq, 2, HD))
        outs[2].append(lat.reshape(bp, seq, KV_LORA))
        outs[3].append(kpe[:, 64:64 + ROPE_B].reshape(bp, seq, ROPE_B))
        wkv4 = wkv.reshape(bp, H, HD, H, HD)
        outs[4].append(jnp.stack([wkv4[:, h, :, h, :] for h in range(H)], axis=1))
        outs[5].append(last[:, 0, :C_SHIFT])
        hp = hp_new

        ps = _proj(hs, ln_g[l], w_packed, db, PROJ_TN)
        qa, ka, _, _ = _moba_prep(ps, prm["gq"], prm["gk"], tabs_a_s, ones_bd, db, 0)
        va = ps[:, P_A + 384:P_A + 512]
        q4 = qa.reshape(db, 2, 2, HD)
        zeros = jnp.zeros((db, HD), F32)
        q8 = jnp.stack([
            jnp.concatenate([q4[:, 0, 0], zeros], axis=1), jnp.concatenate([q4[:, 1, 0], zeros], axis=1),
            jnp.concatenate([zeros, q4[:, 1, 1]], axis=1), jnp.concatenate([zeros, q4[:, 0, 1]], axis=1),
        ] + [jnp.zeros((db, LANES), F32)] * 4, axis=1)
        idx = _moba_pick(page_table, q8, ck, l * n_pool)
        idx12 = idx[:, :H, :MOBA_TOPK].reshape(db, H * MOBA_TOPK)
        o8 = _moba_decode(idx12, page_table, q8, ka, va, ck, cv, l * n_pool)
        oa_s = jnp.concatenate([o8[:, 0, :HD], o8[:, 3, HD:], o8[:, 1, :HD], o8[:, 2, HD:]], axis=1)

        qlat, qpe, snew, lat_s, kpe_s = _mla_prep(ps, prm["gqa"], prm["wuq"], prm["gqb"], prm["gkv"], prm["gkpe"],
                                                  prm["wuk_t"], prm["wuv"], tabs_b_s, db, True)
        pad8 = lambda x: jnp.concatenate([x, jnp.zeros((db, 8 - H) + x.shape[2:], x.dtype)], axis=1)
        qlat8 = pad8(qlat.reshape(db, H, LANES))
        qpe8 = pad8(qpe.reshape(db, H, LANES)[:, :, 64:64 + ROPE_B])
        snew8 = pad8(snew[:, :H].reshape(db, H, 1))
        olat8 = _mla_decode(page_table, qlat8, qpe8, snew8, lat_s, cl, cp, l * n_pool)
        ob_s = _mla_up(olat8[:, :H].reshape(db, H * LANES), prm["wuv_heads"])

        prev = _lane_pad(state_c_shift[l], 0, 1024)
        oc_s, wkv_s = _rwkv_step(ps, prev, state_c_wkv[l], prm["rwkv"], ones_bd, 8)
        hs_new, v_rows = _merge(hs, ln_g[l], w_mg, ps, oa_s, ob_s, oc_s, prm["vn"], prm["ws1"], prm["bias1"],
                                prm["wbr"], prm["wout"], db, 1)
        outs[6].append(ka.reshape(db, 1, 2, HD))
        outs[7].append(va.reshape(db, 1, 2, HD))
        outs[8].append(lat_s.reshape(db, 1, KV_LORA))
        outs[9].append(kpe_s[:, 64:64 + ROPE_B].reshape(db, 1, ROPE_B))
        outs[10].append(wkv_s)
        outs[11].append(ps[:, P_C:P_C + C_SHIFT])
        outs[12].append(v_rows.reshape(db, 1, 256))
        hs = hs_new

    st = lambda xs: jnp.stack(xs, axis=0)
    return (hp.reshape(bp, seq, D_MODEL), hs.reshape(db, 1, D_MODEL)) + tuple(st(o) for o in outs)
```

```python
import functools

import numpy as np
import jax
import jax.numpy as jnp
from jax import lax
from jax.experimental import pallas as pl
from jax.experimental.pallas import tpu as pltpu

F32 = jnp.float32
BF16 = jnp.bfloat16
HIGHEST = lax.Precision.HIGHEST

D_MODEL = 1024
PAGE = 128
ROPE_THETA = 10000.0
NORM_EPS = 1e-6
GN_EPS = 64e-5
HD = 64
H = 4
W_BR = 256
MOBA_BLOCK = 256
MOBA_TOPK = 3
Q_LORA = 256
KV_LORA = 128
NOPE_B = 64
ROPE_B = 32
C_SHIFT = 896
CHUNK_D = 128
CHUNK_C = 64
NEG = -1e30
LANES = 128

P_GATES, P_A, P_B, P_C, P_D, P_COLS = 0, 1024, 1536, 2048, 3072, 3584
PROJ_TN = 1792
N_MERGE = 4 * D_MODEL
O_AQ, O_AK, O_AV, O_AG = 0, 256, 384, 512
O_BDQ, O_BDKV, O_BG = 768, 1024, 1184
O_CX, O_CG = 1440, 2336
O_DU, O_DV, O_DG, O_MG = 2592, 2848, 3104, 3360
A_HEAD_ORDER = (0, 3, 1, 2)

VMEM_LIMIT = 56 * 1024 * 1024


def _cparams(*sem):
    return pltpu.CompilerParams(dimension_semantics=sem, vmem_limit_bytes=VMEM_LIMIT)


def _dot(a, b, precision=None):
    return jnp.dot(a, b, preferred_element_type=F32, precision=precision)


def _dot_nt(a, b, precision=None):
    return lax.dot_general(a, b, (((1,), (1,)), ((), ())), preferred_element_type=F32, precision=precision)


def _dot_tn(a, b, precision=None):
    return lax.dot_general(a, b, (((0,), (0,)), ((), ())), preferred_element_type=F32, precision=precision)


def _seg_sum(x, ones):
    hi = x.astype(BF16)
    lo = (x - hi.astype(F32)).astype(BF16)
    return _dot(hi, ones) + _dot(lo, ones)


def _rope(x, cos, sin_a, sin_b, half):
    return x * cos + pltpu.roll(x, LANES - half, 1) * sin_a + pltpu.roll(x, half, 1) * sin_b


def _sigmoid(x):
    return 1.0 / (1.0 + jnp.exp(-x))


def _silu(x):
    return x * _sigmoid(x)


def _gelu(x):
    return 0.5 * x * (1.0 + jnp.tanh(np.sqrt(2.0 / np.pi).astype(np.float32) * (x + 0.044715 * (x * x * x))))


def _proj_kernel(x_ref, g_ref, w_ref, o_ref, h_ref):
    @pl.when(pl.program_id(1) == 0)
    def _():
        x = x_ref[...]
        y = x * lax.rsqrt(jnp.mean(x * x, axis=-1, keepdims=True) + NORM_EPS)
        h_ref[...] = (y * g_ref[...]).astype(BF16)

    o_ref[...] = _dot(h_ref[...], w_ref[...])


def _proj(x, g, w, tm, tn):
    n, d = x.shape
    c = w.shape[1]
    return pl.pallas_call(
        _proj_kernel,
        grid=(n // tm, c // tn),
        in_specs=[
            pl.BlockSpec((tm, d), lambda i, j: (i, 0)),
            pl.BlockSpec((1, d), lambda i, j: (0, 0)),
            pl.BlockSpec((d, tn), lambda i, j: (0, j)),
        ],
        out_specs=pl.BlockSpec((tm, tn), lambda i, j: (i, j)),
        out_shape=jax.ShapeDtypeStruct((n, c), F32),
        scratch_shapes=[pltpu.VMEM((tm, d), BF16)],
        compiler_params=_cparams("parallel", "arbitrary"),
        name="proj",
    )(x, g.reshape(1, d), w)


def _moba_prep_kernel(a_ref, gq_ref, gk_ref, cos_ref, sa_ref, sb_ref, ones_ref, q_ref, k_ref, kb_ref, vb_ref,
                      *km_ref, blocks, tiles_per_seq):
    a = a_ref[...]
    ones = ones_ref[...]
    cos, sa, sb = cos_ref[...], sa_ref[...], sb_ref[...]
    q, k, v = a[:, :256], a[:, 256:384], a[:, 384:512]
    qn = q * lax.rsqrt(_seg_sum(q * q, ones) * (1.0 / HD) + NORM_EPS) * gq_ref[...]
    kn = k * lax.rsqrt(_seg_sum(k * k, ones[:LANES, :LANES]) * (1.0 / HD) + NORM_EPS) * gk_ref[...]
    q_ref[:, :LANES] = _rope(qn[:, :LANES], cos, sa, sb, HD // 2)
    q_ref[:, LANES:] = _rope(qn[:, LANES:], cos, sa, sb, HD // 2)
    kr = _rope(kn, cos, sa, sb, HD // 2)
    k_ref[...] = kr
    kb_ref[:, :LANES] = kr.astype(BF16)
    vb_ref[...] = v.astype(BF16)
    if blocks:
        km_ref[0][...] = jnp.mean(kr.reshape(blocks, MOBA_BLOCK, LANES), axis=1)
        tm = a.shape[0]
        first_blk = (pl.program_id(0) % tiles_per_seq) * blocks
        blk = first_blk + (lax.broadcasted_iota(jnp.int32, (tm, LANES), 0) >> 8)
        lane = lax.broadcasted_iota(jnp.int32, (tm, LANES), 1)
        kb_ref[:, LANES:] = jnp.where(lane == blk, 1.0, 0.0).astype(BF16)
    else:
        kb_ref[:, LANES:] = jnp.zeros((a.shape[0], LANES), BF16)


def _moba_prep(p, gq, gk, tabs, ones, tm, seq):
    n = p.shape[0]
    row = lambda i: (i, 0)
    fix = lambda i: (0, 0)
    with_kmean = seq > 0
    assert MOBA_BLOCK == 256 and (not with_kmean or seq % tm == 0)
    blocks = tm // MOBA_BLOCK if with_kmean else 0
    tiles_per_seq = max(seq // tm, 1)
    tab = lambda i: (i % tiles_per_seq, 0)
    out_shape = [
        jax.ShapeDtypeStruct((n, 256), F32),
        jax.ShapeDtypeStruct((n, LANES), F32),
        jax.ShapeDtypeStruct((n, 256), BF16),
        jax.ShapeDtypeStruct((n, LANES), BF16),
    ]
    out_specs = [pl.BlockSpec((tm, 256), row), pl.BlockSpec((tm, LANES), row), pl.BlockSpec((tm, 256), row),
                 pl.BlockSpec((tm, LANES), row)]
    if with_kmean:
        out_shape.append(jax.ShapeDtypeStruct((n // MOBA_BLOCK, LANES), F32))
        out_specs.append(pl.BlockSpec((blocks, LANES), row))
    return pl.pallas_call(
        functools.partial(_moba_prep_kernel, blocks=blocks, tiles_per_seq=tiles_per_seq),
        grid=(n // tm,),
        in_specs=[
            pl.BlockSpec((tm, 512), lambda i: (i, P_A // 512)),
            pl.BlockSpec((1, 256), fix),
            pl.BlockSpec((1, LANES), fix),
            pl.BlockSpec((tm, LANES), tab),
            pl.BlockSpec((tm, LANES), tab),
            pl.BlockSpec((tm, LANES), tab),
            pl.BlockSpec((256, 256), fix),
        ],
        out_specs=out_specs,
        out_shape=out_shape,
        compiler_params=_cparams("parallel"),
        name="moba_prep",
    )(p, gq, gk, *tabs, ones)


def _exp_weights(s, m):
    p = jnp.exp((s - m).astype(BF16))
    t = p
    w = t.shape[1]
    while w > LANES:
        w //= 2
        t = t[:, :w] + t[:, w:]
    return p, jnp.sum(t.astype(F32), axis=-1, keepdims=True)


def _softmax_first(s, v):
    m = jnp.max(s, axis=-1, keepdims=True)
    p, ps = _exp_weights(s, m)
    return m, ps, _dot(p, v)


def _softmax_next(s, v, m, l, acc):
    mn = jnp.maximum(m, jnp.max(s, axis=-1, keepdims=True))
    alpha = jnp.exp(m - mn)
    p, ps = _exp_weights(s, mn)
    return mn, alpha * l + ps, alpha * acc + _dot(p, v)


def _moba_attn_kernel(q_ref, k_ref, v_ref, km_ref, o_ref):
    tq = q_ref.shape[0]
    i = pl.program_id(1)
    q = q_ref[...]
    km = km_ref[0]
    lane = lax.broadcasted_iota(jnp.int32, (tq, LANES), 1)
    lower = lane < HD

    nblk = -(-(k_ref.shape[0] // tq) // 8) * 8
    blk = lax.broadcasted_iota(jnp.int32, (nblk, tq), 0)
    blk_f = blk.astype(F32)
    never = jnp.full((LANES - nblk, tq), NEG, F32)
    stacked = []
    for plane in range(2):
        qp = q[:, plane * LANES:(plane + 1) * LANES]
        for half in range(2):
            qm = jnp.where(lower if half == 0 else jnp.logical_not(lower), qp, 0.0)
            gs = jnp.where(blk < i, _dot_nt(km[:nblk], qm, HIGHEST), NEG)
            bias = jnp.where(blk == i, 0.0, NEG)
            for _ in range(MOBA_TOPK):
                m = jnp.max(gs, axis=0, keepdims=True)
                idx = jnp.min(jnp.where(gs == m, blk_f, float(LANES)), axis=0, keepdims=True)
                pick = jnp.logical_and(blk_f == idx, m > 0.5 * NEG)
                bias = jnp.where(pick, 0.0, bias)
                gs = jnp.where(pick, NEG, gs)
            bias = jnp.concatenate([bias, never], axis=0).T if nblk < LANES else bias.T
            stacked.append(jnp.concatenate([(qm * (HD ** -0.5)).astype(BF16), bias.astype(BF16)], axis=1))
    qa = jnp.concatenate(stacked, axis=0)

    tk = 2 * tq
    row = lax.broadcasted_iota(jnp.int32, (H * tq, tk), 0) & (tq - 1)
    col = lax.broadcasted_iota(jnp.int32, (H * tq, tk), 1)
    r0 = pl.multiple_of((i // 2) * tk, tk)
    s = jnp.where(col <= row + (i % 2) * tq, _dot_nt(qa, k_ref[pl.ds(r0, tk), :]), NEG)
    init = _softmax_first(s, v_ref[pl.ds(r0, tk), :])

    def body(j, carry):
        c0 = pl.multiple_of(j * tk, tk)
        return _softmax_next(_dot_nt(qa, k_ref[pl.ds(c0, tk), :]), v_ref[pl.ds(c0, tk), :], *carry)

    _, l, acc = lax.fori_loop(0, i // 2, body, init)
    o = acc / l
    o_ref[:, :LANES] = jnp.where(lower, o[0:tq], o[tq:2 * tq])
    o_ref[:, LANES:] = jnp.where(lower, o[2 * tq:3 * tq], o[3 * tq:4 * tq])


def _moba_attn(q, kb, vb, km, batch, seq):
    nq = seq // MOBA_BLOCK
    return pl.pallas_call(
        _moba_attn_kernel,
        grid=(batch, nq),
        in_specs=[
            pl.BlockSpec((MOBA_BLOCK, 256), lambda b, i: (b * nq + i, 0)),
            pl.BlockSpec((seq, 256), lambda b, i: (b, 0)),
            pl.BlockSpec((seq, LANES), lambda b, i: (b, 0)),
            pl.BlockSpec((1, LANES, LANES), lambda b, i: (b, 0, 0)),
        ],
        out_specs=pl.BlockSpec((MOBA_BLOCK, 256), lambda b, i: (b * nq + i, 0)),
        out_shape=jax.ShapeDtypeStruct((batch * seq, 256), F32),
        compiler_params=_cparams("parallel", "arbitrary"),
        name="moba_attn",
    )(q, kb, vb, km)


def _mla_prep_kernel(b_ref, gqa_ref, wuq_ref, gq_ref, gkv_ref, gkpe_ref, wuk_ref, wuv_ref, cos_ref, sa_ref, sb_ref,
                     *out_refs, decode):
    x = b_ref[...]
    cos, sa, sb = cos_ref[...], sa_ref[...], sb_ref[...]
    dq, lat_raw, kpe_raw = x[:, :256], x[:, 256:384], x[:, 384:512]
    cq = dq * lax.rsqrt(jnp.mean(dq * dq, axis=-1, keepdims=True) + NORM_EPS) * gqa_ref[...]
    q = _dot(cq.astype(BF16), wuq_ref[...])
    lat = lat_raw * lax.rsqrt(jnp.mean(lat_raw * lat_raw, axis=-1, keepdims=True) + NORM_EPS) * gkv_ref[...]
    kpe = kpe_raw * lax.rsqrt(jnp.sum(kpe_raw * kpe_raw, axis=-1, keepdims=True) * (1.0 / ROPE_B) + NORM_EPS)
    kpe = _rope(kpe * gkpe_ref[...], cos, sa, sb, ROPE_B // 2)
    scale = (NOPE_B + ROPE_B) ** -0.5
    qs = []
    for h in range(H):
        qh = q[:, h * LANES:(h + 1) * LANES]
        qh = qh * lax.rsqrt(jnp.sum(qh * qh, axis=-1, keepdims=True) * (1.0 / (NOPE_B + ROPE_B)) + NORM_EPS)
        qs.append(_rope(qh * gq_ref[...], cos, sa, sb, ROPE_B // 2) * scale)
    latb = lat.astype(BF16)
    if decode:
        qlat_ref, qpe_ref, snew_ref, lat_ref, kpe_ref = out_refs
        lane = lax.broadcasted_iota(jnp.int32, kpe.shape, 1)
        snew = jnp.zeros(kpe.shape, F32)
        for h in range(H):
            ql = _dot(qs[h].astype(BF16), wuk_ref[h])
            qlat_ref[:, h * LANES:(h + 1) * LANES] = ql
            qpe_ref[:, h * LANES:(h + 1) * LANES] = qs[h]
            s = jnp.sum(ql.astype(BF16).astype(F32) * latb.astype(F32), axis=-1, keepdims=True)
            s = s + jnp.sum(qs[h].astype(BF16).astype(F32) * kpe.astype(BF16).astype(F32), axis=-1, keepdims=True)
            snew = jnp.where(lane == h, s, snew)
        snew_ref[...] = snew
    else:
        qb_ref, kb_ref, vb_ref, lat_ref, kpe_ref = out_refs
        kn = _dot(latb, wuk_ref[...])
        for h in range(H):
            qb_ref[:, h * LANES:(h + 1) * LANES] = qs[h].astype(BF16)
            kb_ref[:, h * LANES:(h + 1) * LANES] = (kn[:, h * LANES:(h + 1) * LANES] + kpe).astype(BF16)
        vb_ref[...] = _dot(latb, wuv_ref[...]).astype(BF16)
    lat_ref[...] = lat
    kpe_ref[...] = kpe


def _mla_prep(p, gqa, wuq, gq, gkv, gkpe, wuk, wuv, tabs, tm, decode):
    n = p.shape[0]
    row = lambda i: (i, 0)
    fix = lambda i: (0, 0)
    tiles_per_tab = tabs[0].shape[0] // tm
    tab = lambda i: (i % tiles_per_tab, 0)
    wide = lambda dt: (jax.ShapeDtypeStruct((n, 512), dt), pl.BlockSpec((tm, 512), row))
    plane = lambda dt: (jax.ShapeDtypeStruct((n, LANES), dt), pl.BlockSpec((tm, LANES), row))
    if decode:
        outs = [wide(F32), wide(F32), plane(F32), plane(F32), plane(F32)]
        wuk_spec = pl.BlockSpec((H, LANES, LANES), lambda i: (0, 0, 0))
    else:
        outs = [wide(BF16), wide(BF16), (jax.ShapeDtypeStruct((n, 256), BF16), pl.BlockSpec((tm, 256), row)),
                plane(F32), plane(F32)]
        wuk_spec = pl.BlockSpec((LANES, 512), fix)
    return pl.pallas_call(
        functools.partial(_mla_prep_kernel, decode=decode),
        grid=(n // tm,),
        in_specs=[
            pl.BlockSpec((tm, 512), lambda i: (i, P_B // 512)),
            pl.BlockSpec((1, 256), fix),
            pl.BlockSpec((256, 512), fix),
            pl.BlockSpec((1, LANES), fix),
            pl.BlockSpec((1, LANES), fix),
            pl.BlockSpec((1, LANES), fix),
            wuk_spec,
            pl.BlockSpec((LANES, 256), fix),
            pl.BlockSpec((tm, LANES), tab),
            pl.BlockSpec((tm, LANES), tab),
            pl.BlockSpec((tm, LANES), tab),
        ],
        out_specs=[s for _, s in outs],
        out_shape=[s for s, _ in outs],
        compiler_params=_cparams("parallel"),
        name="mla_prep",
    )(p, gqa, wuq, gq, gkv, gkpe, wuk, wuv, *tabs)


def _mla_attn_kernel(q_ref, k_ref, v_ref, o_ref):
    tq = q_ref.shape[0]
    i = pl.program_id(1)
    lane = lax.broadcasted_iota(jnp.int32, (tq, LANES), 1)
    lower = lane < HD
    tk = 2 * tq
    row = lax.broadcasted_iota(jnp.int32, (tq, tk), 0)
    col = lax.broadcasted_iota(jnp.int32, (tq, tk), 1)
    r0 = pl.multiple_of((i // 2) * tk, tk)
    qs = [q_ref[:, h * LANES:(h + 1) * LANES] for h in range(H)]

    def tile(c0, h):
        return (k_ref[pl.ds(c0, tk), h * LANES:(h + 1) * LANES],
                v_ref[pl.ds(c0, tk), (h // 2) * LANES:(h // 2 + 1) * LANES])

    causal = col <= row + (i % 2) * tq
    init = []
    for h in range(H):
        kh, vh = tile(r0, h)
        init.extend(_softmax_first(jnp.where(causal, _dot_nt(qs[h], kh), NEG), vh))

    def body(j, carry):
        c0 = pl.multiple_of(j * tk, tk)
        out = []
        for h in range(H):
            kh, vh = tile(c0, h)
            out.extend(_softmax_next(_dot_nt(qs[h], kh), vh, *carry[3 * h:3 * h + 3]))
        return tuple(out)

    fin = lax.fori_loop(0, i // 2, body, tuple(init))
    for plane in range(2):
        o0 = fin[6 * plane + 2] / fin[6 * plane + 1]
        o1 = fin[6 * plane + 5] / fin[6 * plane + 4]
        o_ref[:, plane * LANES:(plane + 1) * LANES] = jnp.where(lower, o0, o1)


def _mla_attn(qb, kb, vb, batch, seq, tq):
    nq = seq // tq
    return pl.pallas_call(
        _mla_attn_kernel,
        grid=(batch, nq),
        in_specs=[
            pl.BlockSpec((tq, 512), lambda b, i: (b * nq + i, 0)),
            pl.BlockSpec((seq, 512), lambda b, i: (b, 0)),
            pl.BlockSpec((seq, 256), lambda b, i: (b, 0)),
        ],
        out_specs=pl.BlockSpec((tq, 256), lambda b, i: (b * nq + i, 0)),
        out_shape=jax.ShapeDtypeStruct((batch * seq, 256), F32),
        compiler_params=_cparams("parallel", "arbitrary"),
        name="mla_attn",
    )(qb, kb, vb)


def _rwkv_features(xc, prev, mu, w0, wup, a0, aup, kk_w, ka_w, ones):
    xs = xc + (prev - xc) * mu
    r, k, v, lora = xs[:, :256], xs[:, 256:512], xs[:, 512:768], xs[:, 768:896]
    w_pre = w0 + _dot(jnp.tanh(lora).astype(BF16), wup)
    z = -w_pre
    w_log = -(jnp.maximum(z, 0.0) + jnp.log(1.0 + jnp.exp(-jnp.abs(z)))) - 0.5
    lw = -jnp.exp(w_log)
    a = _sigmoid(a0 + _dot(lora.astype(BF16), aup))
    kk = k * kk_w
    kk = kk * lax.rsqrt(_seg_sum(kk * kk, ones) + 1e-12)
    k2 = k * (1.0 + (a - 1.0) * ka_w)
    return r, k2, v, kk, kk * a, lw


def _rwkv_out(y, r, k2, v, rk_w, gn_w, ones):
    mean = _seg_sum(y, ones) * (1.0 / HD)
    d = y - mean
    var = _seg_sum(d * d, ones) * (1.0 / HD)
    return d * lax.rsqrt(var + GN_EPS) * gn_w + _seg_sum(r * k2 * rk_w, ones) * v


def _block_stack(x, same_blk):
    return jnp.where(same_blk, jnp.concatenate([x, x, x, x], axis=0), 0.0)


def _row_blocks_sum(x):
    c = CHUNK_C
    return x[0:c] + x[c:2 * c] + x[2 * c:3 * c] + x[3 * c:4 * c]


def _rwkv_seq_kernel(x_ref, mu_ref, w0_ref, wup_ref, a0_ref, aup_ref, kk_ref, ka_ref, rk_ref, gn_ref, ones_ref,
                     o_ref, s_out_ref, last_ref, s_ref, prev_ref):
    tt = x_ref.shape[0]
    c = CHUNK_C

    @pl.when(pl.program_id(1) == 0)
    def _():
        s_ref[...] = jnp.zeros(s_ref.shape, F32)
        prev_ref[...] = jnp.zeros(prev_ref.shape, F32)

    ones = ones_ref[...]
    xc = x_ref[...]
    rid = lax.broadcasted_iota(jnp.int32, xc.shape, 0)
    prev = jnp.where(rid == 0, prev_ref[...], pltpu.roll(xc, 1, 0))
    prev_ref[...] = xc[tt - 1:tt, :]
    last_ref[0] = xc[tt - 1:tt, :]
    r, k2, v, kk, bv, lw = _rwkv_features(xc, prev, mu_ref[...], w0_ref[...], wup_ref[...], a0_ref[...], aup_ref[...],
                                          kk_ref[...], ka_ref[...], ones)

    rr = lax.broadcasted_iota(jnp.int32, (256, 256), 0)
    cc = lax.broadcasted_iota(jnp.int32, (256, 256), 1)
    same_blk = (rr >> 6) == (cc >> 6)
    strict = (rr & 63) > (cc & 63)
    incl = (rr & 63) >= (cc & 63)
    eye = jnp.where(rr == cc, 1.0, 0.0)
    tri = jnp.where(incl[:c, :c], 1.0, 0.0)

    chunks = []
    for ci in range(tt // c):
        sl = slice(ci * c, (ci + 1) * c)
        lw_c, r_c, k_c, v_c, kk_c, bv_c = lw[sl], r[sl], k2[sl], v[sl], kk[sl], bv[sl]
        cum = _dot(tri, lw_c, HIGHEST)
        last = cum[c - 1:c, :]
        e_neg = jnp.exp(-cum)
        e_end = jnp.exp(last - cum)
        chunks.append(dict(
            a=_block_stack(-kk_c * jnp.exp(cum - lw_c), same_blk).astype(BF16),
            r=_block_stack(r_c * jnp.exp(cum), same_blk).astype(BF16),
            b=_block_stack(bv_c * e_neg, same_blk).astype(BF16),
            k=_block_stack(k_c * e_neg, same_blk).astype(BF16),
            v=_block_stack(v_c, same_blk).astype(BF16),
            b_end=_block_stack(bv_c * e_end, same_blk).astype(BF16),
            k_end=_block_stack(k_c * e_end, same_blk).astype(BF16), decay=jnp.exp(last)))
    for ch in chunks:
        ch["n_ab"] = jnp.where(strict, _dot_nt(ch["a"], ch["b"]), 0.0)
        ch["n_ak"] = jnp.where(strict, _dot_nt(ch["a"], ch["k"]), 0.0).astype(BF16)
        ch["n_rb"] = jnp.where(incl, _dot_nt(ch["r"], ch["b"]), 0.0).astype(BF16)
        ch["n_rk"] = jnp.where(incl, _dot_nt(ch["r"], ch["k"]), 0.0).astype(BF16)
        ch["t_inv"] = eye + ch["n_ab"]
        ch["pw"] = ch["n_ab"].astype(BF16)
    for _ in range(5):
        for ch in chunks:
            ch["pw"] = _dot(ch["pw"], ch["pw"]).astype(BF16)
        for ch in chunks:
            ch["t_inv"] = ch["t_inv"] + _dot(ch["t_inv"].astype(BF16), ch["pw"])
    for ch in chunks:
        t_inv = ch["t_inv"].astype(BF16)
        akv = _dot(ch["n_ak"], ch["v"]).astype(BF16)
        ch["w"] = _dot(t_inv, ch["a"]).astype(BF16)
        ch["u0"] = _dot(t_inv, akv).astype(BF16)
    for ch in chunks:
        ch["r2"] = (ch["r"].astype(F32) + _dot(ch["n_rb"], ch["w"])).astype(BF16)
        ch["y0"] = _dot(ch["n_rb"], ch["u0"]) + _dot(ch["n_rk"], ch["v"])
        ch["m"] = _dot_tn(ch["w"], ch["b_end"]).astype(BF16)
        ch["g"] = _dot_tn(ch["u0"], ch["b_end"]) + _dot_tn(ch["v"], ch["k_end"])

    s = s_ref[...]
    ys = []
    for ch in chunks:
        sb = s.astype(BF16)
        ys.append(_row_blocks_sum(_dot_nt(ch["r2"], sb) + ch["y0"]))
        s = s * ch["decay"] + _dot(sb, ch["m"]) + ch["g"]
    s_ref[...] = s
    s_out_ref[0] = s
    y = jnp.concatenate(ys, axis=0)
    o_ref[...] = _rwkv_out(y, r, k2, v, rk_ref[...], gn_ref[...], ones)


def _rwkv_seq(p, prm, ones, batch, seq, tt):
    nt = seq // tt
    fix = lambda b, t: (0, 0)
    vec256 = pl.BlockSpec((1, 256), fix)
    return pl.pallas_call(
        _rwkv_seq_kernel,
        grid=(batch, nt),
        in_specs=[
            pl.BlockSpec((tt, 1024), lambda b, t: (b * nt + t, P_C // 1024)),
            pl.BlockSpec((1, 1024), fix),
            vec256, pl.BlockSpec((LANES, 256), fix), vec256, pl.BlockSpec((LANES, 256), fix),
            vec256, vec256, vec256, vec256,
            pl.BlockSpec((256, 256), fix),
        ],
        out_specs=[
            pl.BlockSpec((tt, 256), lambda b, t: (b * nt + t, 0)),
            pl.BlockSpec((1, 256, 256), lambda b, t: (b, 0, 0)),
            pl.BlockSpec((1, 1, 1024), lambda b, t: (b, 0, 0)),
        ],
        out_shape=[
            jax.ShapeDtypeStruct((batch * seq, 256), F32),
            jax.ShapeDtypeStruct((batch, 256, 256), F32),
            jax.ShapeDtypeStruct((batch, 1, 1024), F32),
        ],
        scratch_shapes=[pltpu.VMEM((256, 256), F32), pltpu.VMEM((1, 1024), F32)],
        compiler_params=_cparams("parallel", "arbitrary"),
        name="rwkv_seq",
    )(p, prm["mu"], prm["w0"], prm["wup"], prm["a0"], prm["aup"], prm["kk"], prm["ka"], prm["rk"], prm["gn"], ones)


def _rwkv_step_kernel(x_ref, prev_ref, s_ref, mu_ref, w0_ref, wup_ref, a0_ref, aup_ref, kk_ref, ka_ref, rk_ref,
                      gn_ref, ones_ref, o_ref, s_out_ref, yt_ref):
    nb = x_ref.shape[0]
    ones = ones_ref[...]
    r, k2, v, kk, bv, lw = _rwkv_features(x_ref[...], prev_ref[...], mu_ref[...], w0_ref[...], wup_ref[...],
                                          a0_ref[...], aup_ref[...], kk_ref[...], ka_ref[...], ones)
    w = jnp.exp(lw)
    vt = v.T
    for b in range(nb):
        for h in range(H):
            hs = slice(h * HD, (h + 1) * HD)
            st = s_ref[b, h]
            sa = -jnp.sum(st * kk[b:b + 1, hs], axis=-1, keepdims=True)
            st = st * w[b:b + 1, hs] + sa * bv[b:b + 1, hs] + vt[hs, b:b + 1] * k2[b:b + 1, hs]
            s_out_ref[b, h] = st
            yt_ref[hs, b:b + 1] = jnp.sum(st * r[b:b + 1, hs], axis=-1, keepdims=True)
    y = yt_ref[...].T
    o_ref[...] = _rwkv_out(y, r, k2, v, rk_ref[...], gn_ref[...], ones)


def _rwkv_step(p, prev, state, prm, ones, nb):
    n = p.shape[0]
    fix = lambda i: (0, 0)
    vec256 = pl.BlockSpec((1, 256), fix)
    return pl.pallas_call(
        _rwkv_step_kernel,
        grid=(n // nb,),
        in_specs=[
            pl.BlockSpec((nb, 1024), lambda i: (i, P_C // 1024)),
            pl.BlockSpec((nb, 1024), lambda i: (i, 0)),
            pl.BlockSpec((nb, H, HD, HD), lambda i: (i, 0, 0, 0)),
            pl.BlockSpec((1, 1024), fix),
            vec256, pl.BlockSpec((LANES, 256), fix), vec256, pl.BlockSpec((LANES, 256), fix),
            vec256, vec256, vec256, vec256,
            pl.BlockSpec((256, 256), fix),
        ],
        out_specs=[
            pl.BlockSpec((nb, 256), lambda i: (i, 0)),
            pl.BlockSpec((nb, H, HD, HD), lambda i: (i, 0, 0, 0)),
        ],
        out_shape=[
            jax.ShapeDtypeStruct((n, 256), F32),
            jax.ShapeDtypeStruct((n, H, HD, HD), F32),
        ],
        scratch_shapes=[pltpu.VMEM((256, nb), F32)],
        compiler_params=_cparams("parallel"),
        name="rwkv_step",
    )(p, prev, state, prm["mu"], prm["w0"], prm["wup"], prm["a0"], prm["aup"], prm["kk"], prm["ka"], prm["rk"],
      prm["gn"], ones)


def _merge_kernel(x_ref, lng_ref, wmg_ref, g_ref, oa_ref, ob_ref, oc_ref, d_ref, vn_ref, ws_ref, bias_ref, wbr_ref,
                  wout_ref, o_ref, *v_ref, chunk):
    tm = x_ref.shape[0]
    x = x_ref[...]
    hb = (x * lax.rsqrt(jnp.mean(x * x, axis=-1, keepdims=True) + NORM_EPS) * lng_ref[...]).astype(BF16)
    d = d_ref[...]
    u = _gelu(d[:, :256])
    vg = _gelu(d[:, 256:512])
    v = vg * lax.rsqrt(jnp.mean(vg * vg, axis=-1, keepdims=True) + NORM_EPS) * vn_ref[...]
    if chunk == 1:
        sp = v * ws_ref[...] + bias_ref[...]
        v_ref[0][...] = v
    else:
        rr = lax.broadcasted_iota(jnp.int32, (chunk, chunk), 0)
        cc = lax.broadcasted_iota(jnp.int32, (chunk, chunk), 1)
        grp = lax.broadcasted_iota(jnp.int32, (chunk, 256), 1) >> 6
        vb = v.astype(BF16)
        ws = [jnp.where(cc <= rr, ws_ref[g], 0.0).astype(BF16) for g in range(H)]
        parts = []
        for ci in range(tm // chunk):
            vc = vb[ci * chunk:(ci + 1) * chunk]
            acc = bias_ref[...]
            for g in range(H):
                acc = acc + jnp.where(grp == g, _dot(ws[g], vc), 0.0)
            parts.append(acc)
        sp = jnp.concatenate(parts, axis=0)
    od = u * sp
    gates = g_ref[...]
    y = jnp.zeros((tm, D_MODEL), F32)
    for bi, o in enumerate((oa_ref[...], ob_ref[...], oc_ref[...], od)):
        z = (o * _silu(gates[:, bi * 256:(bi + 1) * 256])).astype(BF16)
        mg = _dot(hb, wmg_ref[:, bi * D_MODEL:(bi + 1) * D_MODEL])
        y = y + _sigmoid(mg) * _dot(z, wbr_ref[bi])
    o_ref[...] = x + _dot(y.astype(BF16), wout_ref[...])


def _merge(x, lng, wmg, p, oa, ob, oc, vn, ws, bias, wbr, wout, tm, chunk):
    n = x.shape[0]
    row = lambda i: (i, 0)
    fix = lambda i: (0, 0)
    out_shape = [jax.ShapeDtypeStruct((n, D_MODEL), F32)]
    out_specs = [pl.BlockSpec((tm, D_MODEL), row)]
    if chunk == 1:
        out_shape.append(jax.ShapeDtypeStruct((n, 256), F32))
        out_specs.append(pl.BlockSpec((tm, 256), row))
        ws_spec = pl.BlockSpec((1, 256), fix)
        bias_spec = pl.BlockSpec((1, 256), fix)
    else:
        ws_spec = pl.BlockSpec((H, chunk, chunk), lambda i: (0, 0, 0))
        bias_spec = pl.BlockSpec((chunk, 256), fix)
    return pl.pallas_call(
        functools.partial(_merge_kernel, chunk=chunk),
        grid=(n // tm,),
        in_specs=[
            pl.BlockSpec((tm, D_MODEL), row),
            pl.BlockSpec((1, D_MODEL), fix),
            pl.BlockSpec((D_MODEL, N_MERGE), fix),
            pl.BlockSpec((tm, 1024), lambda i: (i, P_GATES // 1024)),
            pl.BlockSpec((tm, 256), row),
            pl.BlockSpec((tm, 256), row),
            pl.BlockSpec((tm, 256), row),
            pl.BlockSpec((tm, 512), lambda i: (i, P_D // 512)),
            pl.BlockSpec((1, 256), fix),
            ws_spec,
            bias_spec,
            pl.BlockSpec((H, 256, D_MODEL), lambda i: (0, 0, 0)),
            pl.BlockSpec((D_MODEL, D_MODEL), fix),
        ],
        out_specs=out_specs,
        out_shape=out_shape,
        compiler_params=_cparams("parallel"),
        name="merge",
    )(x, lng.reshape(1, D_MODEL), wmg, p, oa, ob, oc, p, vn, ws, bias, wbr, wout)


def _page_pipeline(b, n_rows, copies_of, n_copies):
    slot = b % 2

    def start_row(row, sl):
        for i in range(n_copies):
            for j, cp in enumerate(copies_of(row, sl, i)):
                cp.start(priority=(i + j) % 2)

    @pl.when(b == 0)
    def _():
        start_row(0, 0)

    for i in range(n_copies):
        for cp in copies_of(b, slot, i):
            cp.wait()
    start_row(jnp.minimum(b + 1, n_rows - 1), 1 - slot)
    return slot


def _page_drain(b, n_rows, copies_of, n_copies):
    @pl.when(b == n_rows - 1)
    def _():
        for i in range(n_copies):
            for cp in copies_of(b, 1 - b % 2, i):
                cp.wait()


def _moba_pick_kernel(pt_ref, q_ref, k_hbm, idx_ref, k_buf, sem, *, base, n_pages):
    b = pl.program_id(0)

    def copies_of(row, sl, i):
        return (pltpu.make_async_copy(k_hbm.at[base + pt_ref[row, i]], k_buf.at[sl, :, pl.ds(i * PAGE, PAGE)],
                                      sem.at[sl]),)

    slot = _page_pipeline(b, pl.num_programs(0), copies_of, n_pages)
    nblk = n_pages * PAGE // MOBA_BLOCK
    q = q_ref[0]
    hi = q.astype(BF16).astype(F32)
    q2 = jnp.concatenate([hi, q - hi], axis=0).astype(BF16)
    s2 = _dot(q2, k_buf[slot].astype(BF16))
    s = s2[:8] + s2[8:]
    lane = lax.broadcasted_iota(jnp.int32, (8, LANES), 1)
    lane_f = lane.astype(F32)
    gs = jnp.full((8, LANES), NEG, F32)
    for blk in range(nblk):
        tot = jnp.sum(s[:, blk * MOBA_BLOCK:(blk + 1) * MOBA_BLOCK], axis=-1, keepdims=True)
        gs = jnp.where(lane == blk, tot * (1.0 / MOBA_BLOCK), gs)
    out = jnp.zeros((8, LANES), jnp.int32)
    for t in range(MOBA_TOPK):
        m = jnp.max(gs, axis=1, keepdims=True)
        idx = jnp.min(jnp.where(gs == m, lane_f, float(LANES)), axis=1, keepdims=True)
        out = jnp.where(lane == t, idx.astype(jnp.int32), out)
        gs = jnp.where(lane_f == idx, NEG, gs)
    idx_ref[0] = out
    _page_drain(b, pl.num_programs(0), copies_of, n_pages)


def _moba_pick(page_table, q8, cache_kt, layer_base):
    nb, n_pages = page_table.shape
    assert n_pages * PAGE // MOBA_BLOCK <= LANES
    grid_spec = pltpu.PrefetchScalarGridSpec(
        num_scalar_prefetch=1,
        grid=(nb,),
        in_specs=[pl.BlockSpec((1, 8, LANES), lambda b, pt: (b, 0, 0)), pl.BlockSpec(memory_space=pl.ANY)],
        out_specs=pl.BlockSpec((1, 8, LANES), lambda b, pt: (b, 0, 0)),
        scratch_shapes=[pltpu.VMEM((2, LANES, n_pages * PAGE), F32), pltpu.SemaphoreType.DMA((2,))],
    )
    return pl.pallas_call(
        functools.partial(_moba_pick_kernel, base=layer_base, n_pages=n_pages),
        grid_spec=grid_spec,
        out_shape=jax.ShapeDtypeStruct((nb, 8, LANES), jnp.int32),
        compiler_params=_cparams("arbitrary"),
        name="moba_pick",
    )(page_table, q8, cache_kt)


def _moba_decode_kernel(idx_ref, pt_ref, q_ref, kn_ref, vn_ref, k_hbm, v_hbm, o_ref, k_buf, v_buf, sem, *, base):
    b = pl.program_id(0)
    per_blk = MOBA_BLOCK // PAGE
    per_head = MOBA_TOPK * per_blk

    def copies_of(row, sl, i):
        out = []
        for h in range(H):
            for t in range(MOBA_TOPK):
                for pi in range(per_blk):
                    page = base + pt_ref[row, idx_ref[row, h * MOBA_TOPK + t] * per_blk + pi]
                    dst = pl.ds((t * per_blk + pi) * PAGE, PAGE)
                    out.append(pltpu.make_async_copy(k_hbm.at[page], k_buf.at[sl, h, :, dst], sem.at[0, sl]))
                    out.append(pltpu.make_async_copy(v_hbm.at[page], v_buf.at[sl, h, :, dst], sem.at[1, sl]))
        return out

    slot = _page_pipeline(b, pl.num_programs(0), copies_of, 1)
    q = q_ref[0]
    qb = (q * (HD ** -0.5)).astype(BF16)
    kn = kn_ref[0].astype(BF16).astype(F32)
    vn = vn_ref[0].astype(BF16).astype(F32)
    s_own = jnp.sum(qb.astype(F32) * kn, axis=-1, keepdims=True)
    rowid = lax.broadcasted_iota(jnp.int32, (8, LANES), 0)
    out = jnp.zeros((8, LANES), F32)
    for h in range(H):
        s = _dot(qb, k_buf[slot, h].astype(BF16))
        m = jnp.maximum(s_own, jnp.max(s, axis=-1, keepdims=True))
        p_own = jnp.exp(s_own - m)
        p = jnp.exp(s - m)
        l = p_own + jnp.sum(p, axis=-1, keepdims=True)
        acc = p_own.astype(BF16).astype(F32) * vn + _dot_nt(p.astype(BF16), v_buf[slot, h].astype(BF16))
        out = jnp.where(rowid == h, acc / l, out)
    o_ref[0] = out
    _page_drain(b, pl.num_programs(0), copies_of, 1)


def _moba_decode(idx, page_table, q8, k_new, v_new, cache_kt, cache_vt, layer_base):
    nb = page_table.shape[0]
    sel_tokens = MOBA_TOPK * MOBA_BLOCK
    vec = pl.BlockSpec((1, 1, LANES), lambda b, idx_r, pt: (b, 0, 0))
    grid_spec = pltpu.PrefetchScalarGridSpec(
        num_scalar_prefetch=2,
        grid=(nb,),
        in_specs=[pl.BlockSpec((1, 8, LANES), lambda b, idx_r, pt: (b, 0, 0)), vec, vec,
                  pl.BlockSpec(memory_space=pl.ANY), pl.BlockSpec(memory_space=pl.ANY)],
        out_specs=pl.BlockSpec((1, 8, LANES), lambda b, idx_r, pt: (b, 0, 0)),
        scratch_shapes=[pltpu.VMEM((2, H, LANES, sel_tokens), F32), pltpu.VMEM((2, H, LANES, sel_tokens), F32),
                        pltpu.SemaphoreType.DMA((2, 2))],
    )
    return pl.pallas_call(
        functools.partial(_moba_decode_kernel, base=layer_base),
        grid_spec=grid_spec,
        out_shape=jax.ShapeDtypeStruct((nb, 8, LANES), F32),
        compiler_params=_cparams("arbitrary"),
        name="moba_decode",
    )(idx, page_table, q8, k_new.reshape(nb, 1, LANES), v_new.reshape(nb, 1, LANES), cache_kt, cache_vt)


def _mla_decode_kernel(pt_ref, ql_ref, qp_ref, sn_ref, ln_ref, lat_hbm, kpe_hbm, o_ref, lat_buf, kpe_buf, sem, *,
                       base, n_pages):
    b = pl.program_id(0)

    def copies_of(row, sl, i):
        page = base + pt_ref[row, i]
        off = pl.ds(i * PAGE, PAGE)
        return (pltpu.make_async_copy(lat_hbm.at[page], lat_buf.at[sl, off, :], sem.at[0, sl]),
                pltpu.make_async_copy(kpe_hbm.at[page], kpe_buf.at[sl, :, off], sem.at[1, sl]))

    slot = _page_pipeline(b, pl.num_programs(0), copies_of, n_pages)
    ql = ql_ref[0].astype(BF16)
    qp = qp_ref[0].astype(BF16)
    lat = lat_buf[slot].astype(BF16)
    s = _dot_nt(ql, lat) + _dot(qp, kpe_buf[slot].astype(BF16))
    s_new = sn_ref[0]
    m = jnp.maximum(s_new, jnp.max(s, axis=-1, keepdims=True))
    p_new = jnp.exp(s_new - m)
    p = jnp.exp(s - m).astype(BF16)
    l = p_new + jnp.sum(p.astype(F32), axis=-1, keepdims=True)
    acc = p_new.astype(BF16).astype(F32) * ln_ref[0].astype(BF16).astype(F32) + _dot(p, lat)
    o_ref[0] = acc / l
    _page_drain(b, pl.num_programs(0), copies_of, n_pages)


def _mla_decode(page_table, qlat8, qpe8, snew8, lat_new, cache_lat, cache_kpet, layer_base):
    nb, n_pages = page_table.shape
    per_b = lambda shape: pl.BlockSpec((1,) + shape, lambda b, pt: (b, 0, 0))
    grid_spec = pltpu.PrefetchScalarGridSpec(
        num_scalar_prefetch=1,
        grid=(nb,),
        in_specs=[per_b((8, LANES)), per_b((8, ROPE_B)), per_b((8, 1)), per_b((1, LANES)),
                  pl.BlockSpec(memory_space=pl.ANY), pl.BlockSpec(memory_space=pl.ANY)],
        out_specs=per_b((8, LANES)),
        scratch_shapes=[pltpu.VMEM((2, n_pages * PAGE, KV_LORA), F32), pltpu.VMEM((2, ROPE_B, n_pages * PAGE), F32),
                        pltpu.SemaphoreType.DMA((2, 2))],
    )
    return pl.pallas_call(
        functools.partial(_mla_decode_kernel, base=layer_base, n_pages=n_pages),
        grid_spec=grid_spec,
        out_shape=jax.ShapeDtypeStruct((nb, 8, LANES), F32),
        compiler_params=_cparams("arbitrary"),
        name="mla_decode",
    )(page_table, qlat8, qpe8, snew8, lat_new.reshape(nb, 1, LANES), cache_lat, cache_kpet)


def _mla_up_kernel(o_ref, w_ref, out_ref):
    acc = jnp.zeros(out_ref.shape, F32)
    for h in range(H):
        acc = acc + _dot(o_ref[:, h * LANES:(h + 1) * LANES].astype(BF16), w_ref[h])
    out_ref[...] = acc


def _mla_up(o_lat, wuv_heads):
    n = o_lat.shape[0]
    return pl.pallas_call(
        _mla_up_kernel,
        out_shape=jax.ShapeDtypeStruct((n, 256), F32),
        name="mla_up",
    )(o_lat, wuv_heads)


def _pack_w_in(w):
    d = w.shape[0]
    z = lambda n: jnp.zeros((d, n), w.dtype)
    col = lambda o, n: w[:, o:o + n]
    aq = jnp.concatenate([col(O_AQ + h * HD, HD) for h in A_HEAD_ORDER], axis=1)
    ag = jnp.concatenate([col(O_AG + h * HD, HD) for h in A_HEAD_ORDER], axis=1)
    parts = [
        ag, col(O_BG, 256), col(O_CG, 256), col(O_DG, 256),
        aq, col(O_AK, 128), col(O_AV, 128),
        col(O_BDQ, 256), col(O_BDKV, KV_LORA), z(64), col(O_BDKV + KV_LORA, ROPE_B), z(32),
        col(O_CX, C_SHIFT), z(128),
        col(O_DU, 256), col(O_DV, 256),
    ]
    out = jnp.concatenate(parts, axis=1).astype(BF16)
    assert out.shape[1] == P_COLS
    return out, col(O_MG, N_MERGE).astype(BF16)


def _rope_tables(pos, lane_lo, dim):
    half = dim // 2
    inv = ROPE_THETA ** (-jnp.arange(half, dtype=F32) / half)
    ang = pos.astype(F32)[:, None] * inv[None, :]
    cos, sin = jnp.cos(ang), jnp.sin(ang)
    lane = np.arange(LANES)
    span = LANES if dim == HD else dim
    active = (lane >= lane_lo) & (lane < lane_lo + span)
    rel = (lane - lane_lo) % dim
    first = rel < half
    gather = rel % half
    cos_t = jnp.where(active[None, :], cos[:, gather], 1.0)
    sin_t = sin[:, gather]
    sin_a = jnp.where((active & first)[None, :], -sin_t, 0.0)
    sin_b = jnp.where((active & ~first)[None, :], sin_t, 0.0)
    return cos_t, sin_a, sin_b


def _lane_pad(x, lo, width=LANES):
    out = jnp.zeros(x.shape[:-1] + (width,), x.dtype)
    return out.at[..., lo:lo + x.shape[-1]].set(x)


def _layer_params(l, a_q_norm, a_k_norm, b_q_a_norm, b_w_uq, b_q_norm, b_kv_a_norm, b_kpe_norm, b_w_uk, b_w_uv,
                  c_mu, c_w0, c_w_up, c_a0, c_a_up, c_k_k, c_k_a, c_r_k, c_gn, d_v_norm, d_ws, d_b,
                  w_br_a, w_br_b, w_br_c, w_br_d, w_out):
    prm = {}
    prm["gq"] = jnp.tile(a_q_norm[l], 4).reshape(1, 256)
    prm["gk"] = jnp.tile(a_k_norm[l], 2).reshape(1, LANES)
    prm["gqa"] = b_q_a_norm[l].reshape(1, Q_LORA)
    prm["wuq"] = _lane_pad(b_w_uq[l], 0).reshape(Q_LORA, H * LANES).astype(BF16)
    prm["gqb"] = _lane_pad(b_q_norm[l], 0).reshape(1, LANES)
    prm["gkv"] = b_kv_a_norm[l].reshape(1, KV_LORA)
    prm["gkpe"] = _lane_pad(b_kpe_norm[l], 64).reshape(1, LANES)
    prm["wuk"] = _lane_pad(b_w_uk[l], 0).reshape(KV_LORA, H * LANES).astype(BF16)
    wukt = jnp.transpose(b_w_uk[l], (1, 2, 0))
    prm["wuk_t"] = jnp.concatenate([wukt, jnp.zeros((H, LANES - NOPE_B, KV_LORA), F32)], axis=1).astype(BF16)
    prm["wuv"] = b_w_uv[l].reshape(KV_LORA, H * HD).astype(BF16)
    wuv_h = jnp.zeros((H, KV_LORA, 256), F32)
    for h in range(H):
        wuv_h = wuv_h.at[h, :, h * HD:(h + 1) * HD].set(b_w_uv[l][:, h, :])
    prm["wuv_heads"] = wuv_h.astype(BF16)
    rw = {}
    rw["mu"] = _lane_pad(c_mu[l], 0, 1024).reshape(1, 1024)
    rw["w0"] = c_w0[l].reshape(1, 256)
    rw["wup"] = jnp.concatenate([c_w_up[l], jnp.zeros((64, 256), F32)], axis=0).astype(BF16)
    rw["a0"] = c_a0[l].reshape(1, 256)
    rw["aup"] = jnp.concatenate([jnp.zeros((64, 256), F32), c_a_up[l]], axis=0).astype(BF16)
    rw["kk"] = c_k_k[l].reshape(1, 256)
    rw["ka"] = c_k_a[l].reshape(1, 256)
    rw["rk"] = c_r_k[l].reshape(1, 256)
    rw["gn"] = c_gn[l].reshape(1, 256)
    prm["rwkv"] = rw
    prm["vn"] = d_v_norm[l].reshape(1, 256)
    prm["ws"] = d_ws[l]
    prm["bias"] = jnp.repeat(d_b[l].T, HD, axis=1)
    prm["ws1"] = jnp.repeat(d_ws[l][:, 0, 0], HD).reshape(1, 256)
    prm["bias1"] = jnp.repeat(d_b[l][:, 0], HD).reshape(1, 256)
    wa = w_br_a[l].reshape(H, HD, D_MODEL)[jnp.array(A_HEAD_ORDER)].reshape(256, D_MODEL)
    prm["wbr"] = jnp.stack([wa, w_br_b[l], w_br_c[l], w_br_d[l]], axis=0).astype(BF16)
    prm["wout"] = w_out[l].astype(BF16)
    return prm


def kernel(x_prompt, x_sample, cache_a_k, cache_a_v, cache_b_latent, cache_b_kpe, state_c_wkv, state_c_shift, page_table, ln_g, w_in, a_q_norm, a_k_norm, b_q_a_norm, b_w_uq, b_q_norm, b_kv_a_norm, b_kpe_norm, b_w_uk, b_w_uv, c_mu, c_w0, c_w_up, c_a0, c_a_up, c_k_k, c_k_a, c_r_k, c_gn, d_v_norm, d_ws, d_b, w_br_a, w_br_b, w_br_c, w_br_d, w_out):
    depth = w_in.shape[0]
    bp, seq, _ = x_prompt.shape
    db, dec_t, _ = x_sample.shape
    n_pool = cache_a_k.shape[1]
    n_pages = page_table.shape[1]
    past = n_pages * PAGE
    assert dec_t == 1 and past % MOBA_BLOCK == 0 and past // MOBA_BLOCK >= MOBA_TOPK
    assert seq % (2 * MOBA_BLOCK) == 0 and seq // MOBA_BLOCK <= LANES
    n = bp * seq
    nblk = seq // MOBA_BLOCK

    ones_bd = jnp.asarray(np.kron(np.eye(4), np.ones((HD, HD))), BF16)
    pos_p = jnp.arange(seq, dtype=jnp.int32)
    pos_s = jnp.full((db,), past, jnp.int32)
    tabs_a_p, tabs_a_s = _rope_tables(pos_p, 0, HD), _rope_tables(pos_s, 0, HD)
    tabs_b_p, tabs_b_s = _rope_tables(pos_p, 64, ROPE_B), _rope_tables(pos_s, 64, ROPE_B)

    ck = jnp.transpose(cache_a_k, (0, 1, 3, 4, 2)).reshape(depth * n_pool, LANES, PAGE)
    cv = jnp.transpose(cache_a_v, (0, 1, 3, 4, 2)).reshape(depth * n_pool, LANES, PAGE)
    cl = cache_b_latent.reshape(depth * n_pool, PAGE, KV_LORA)
    cp = jnp.transpose(cache_b_kpe, (0, 1, 3, 2)).reshape(depth * n_pool, ROPE_B, PAGE)

    tm_prompt = min(1024, n)
    tm_prep = min(2048, seq)
    hp = x_prompt.reshape(n, D_MODEL)
    hs = x_sample.reshape(db, D_MODEL)
    outs = [[] for _ in range(13)]
    for l in range(depth):
        prm = _layer_params(l, a_q_norm, a_k_norm, b_q_a_norm, b_w_uq, b_q_norm, b_kv_a_norm, b_kpe_norm, b_w_uk,
                            b_w_uv, c_mu, c_w0, c_w_up, c_a0, c_a_up, c_k_k, c_k_a, c_r_k, c_gn, d_v_norm, d_ws, d_b,
                            w_br_a, w_br_b, w_br_c, w_br_d, w_out)
        w_packed, w_mg = _pack_w_in(w_in[l])

        pp = _proj(hp, ln_g[l], w_packed, tm_prompt, PROJ_TN)
        qa, ka, kab, vab, kmean = _moba_prep(pp, prm["gq"], prm["gk"], tabs_a_p, ones_bd, tm_prep, seq)
        km_pad = jnp.zeros((bp, LANES, LANES), F32).at[:, :nblk].set(kmean.reshape(bp, nblk, LANES))
        oa = _moba_attn(qa, kab, vab, km_pad, bp, seq)
        qbb, kbb, vbb, lat, kpe = _mla_prep(pp, prm["gqa"], prm["wuq"], prm["gqb"], prm["gkv"], prm["gkpe"],
                                            prm["wuk"], prm["wuv"], tabs_b_p, tm_prep, False)
        ob = _mla_attn(qbb, kbb, vbb, bp, seq, min(512, seq))
        oc, wkv, last = _rwkv_seq(pp, prm["rwkv"], ones_bd, bp, seq, min(512, seq))
        (hp_new,) = _merge(hp, ln_g[l], w_mg, pp, oa, ob, oc, prm["vn"], prm["ws"], prm["bias"], prm["wbr"],
                           prm["wout"], min(256, n), CHUNK_D)
        outs[0].append(ka.reshape(bp, seq, 2, HD))
        outs[1].append(pp[:, P_A + 384:P_A + 512].reshape(bp, seq, 2, HD))
        outs[2].append(lat.reshape(bp, seq, KV_LORA))
        outs[3].append(kpe[:, 64:64 + ROPE_B].reshape(bp, seq, ROPE_B))
        wkv4 = wkv.reshape(bp, H, HD, H, HD)
        outs[4].append(jnp.stack([wkv4[:, h, :, h, :] for h in range(H)], axis=1))
        outs[5].append(last[:, 0, :C_SHIFT])
        hp = hp_new

        ps = _proj(hs, ln_g[l], w_packed, db, PROJ_TN)
        qa, ka, _, _ = _moba_prep(ps, prm["gq"], prm["gk"], tabs_a_s, ones_bd, db, 0)
        va = ps[:, P_A + 384:P_A + 512]
        q4 = qa.reshape(db, 2, 2, HD)
        zeros = jnp.zeros((db, HD), F32)
        q8 = jnp.stack([
            jnp.concatenate([q4[:, 0, 0], zeros], axis=1), jnp.concatenate([q4[:, 1, 0], zeros], axis=1),
            jnp.concatenate([zeros, q4[:, 1, 1]], axis=1), jnp.concatenate([zeros, q4[:, 0, 1]], axis=1),
        ] + [jnp.zeros((db, LANES), F32)] * 4, axis=1)
        idx = _moba_pick(page_table, q8, ck, l * n_pool)
        idx12 = idx[:, :H, :MOBA_TOPK].reshape(db, H * MOBA_TOPK)
        o8 = _moba_decode(idx12, page_table, q8, ka, va, ck, cv, l * n_pool)
        oa_s = jnp.concatenate([o8[:, 0, :HD], o8[:, 3, HD:], o8[:, 1, :HD], o8[:, 2, HD:]], axis=1)

        qlat, qpe, snew, lat_s, kpe_s = _mla_prep(ps, prm["gqa"], prm["wuq"], prm["gqb"], prm["gkv"], prm["gkpe"],
                                                  prm["wuk_t"], prm["wuv"], tabs_b_s, db, True)
        pad8 = lambda x: jnp.concatenate([x, jnp.zeros((db, 8 - H) + x.shape[2:], x.dtype)], axis=1)
        qlat8 = pad8(qlat.reshape(db, H, LANES))
        qpe8 = pad8(qpe.reshape(db, H, LANES)[:, :, 64:64 + ROPE_B])
        snew8 = pad8(snew[:, :H].reshape(db, H, 1))
        olat8 = _mla_decode(page_table, qlat8, qpe8, snew8, lat_s, cl, cp, l * n_pool)
        ob_s = _mla_up(olat8[:, :H].reshape(db, H * LANES), prm["wuv_heads"])

        prev = _lane_pad(state_c_shift[l], 0, 1024)
        oc_s, wkv_s = _rwkv_step(ps, prev, state_c_wkv[l], prm["rwkv"], ones_bd, 8)
        hs_new, v_rows = _merge(hs, ln_g[l], w_mg, ps, oa_s, ob_s, oc_s, prm["vn"], prm["ws1"], prm["bias1"],
                                prm["wbr"], prm["wout"], db, 1)
        outs[6].append(ka.reshape(db, 1, 2, HD))
        outs[7].append(va.reshape(db, 1, 2, HD))
        outs[8].append(lat_s.reshape(db, 1, KV_LORA))
        outs[9].append(kpe_s[:, 64:64 + ROPE_B].reshape(db, 1, ROPE_B))
        outs[10].append(wkv_s)
        outs[11].append(ps[:, P_C:P_C + C_SHIFT])
        outs[12].append(v_rows.reshape(db, 1, 256))
        hs = hs_new

    st = lambda xs: jnp.stack(xs, axis=0)
    return (hp.reshape(bp, seq, D_MODEL), hs.reshape(db, 1, D_MODEL)) + tuple(st(o) for o in outs)
```
